```python
import math, functools
import jax, jax.numpy as jnp
from jax import lax
import numpy as np

D_MODEL = 2048
BATCH = 2
SEQ = 4096
DEPTH = 2
DEC_BATCH = 8
DEC_SEQ = 8
PAST_LEN = 16384
PAGE_SIZE = 128

N_META = 16
N_HEADS = 8
QK_DIM = D_MODEL // (4 * N_HEADS)
V_DIM = 2 * QK_DIM
ATTN_WIDTH = N_HEADS * V_DIM
QK_COLS = N_HEADS * 2 * QK_DIM
V_COLS = N_HEADS * V_DIM
POOL_WIDTH = D_MODEL // 4
POOL_WINDOWS = (2, 4, 8, 16)
POOL_GROUP = POOL_WIDTH // len(POOL_WINDOWS)
POOL_HIST = max(POOL_WINDOWS) - 1
CONV_CH = D_MODEL - ATTN_WIDTH - POOL_WIDTH
CONV_K = 3
MIX_WIDTH = ATTN_WIDTH + POOL_WIDTH + CONV_CH
PROJ_WIDTH = 2 * QK_COLS + V_COLS + POOL_WIDTH + 3 * CONV_CH
FFN_HIDDEN = -(-(8 * D_MODEL) // (3 * 256)) * 256
Q_BLOCK = 128
EPS = 1e-6
NEG_INF = -1e30

kernel_name = "hymba_diffattn_pool_shortconv_decode_step"


def rmsnorm(x, g):
    xf = x.astype(jnp.float32)
    r = lax.rsqrt(jnp.mean(xf * xf, axis=-1, keepdims=True) + EPS)
    return (xf * r * g.astype(jnp.float32)).astype(x.dtype)


def alibi_slopes():
    return jnp.asarray(2.0 ** (-8.0 * (np.arange(N_HEADS) + 1) / N_HEADS), dtype=jnp.float32)


def diff_attn_prompt(q, k, v, lam):
    b, L = q.shape[0], q.shape[1]
    n_blk = -(-L // Q_BLOCK)
    lp = n_blk * Q_BLOCK
    pad = lp - L
    qp = jnp.pad(q, ((0, 0), (0, pad), (0, 0), (0, 0), (0, 0)))
    kp = jnp.pad(k, ((0, 0), (0, pad), (0, 0), (0, 0), (0, 0)))
    vp = jnp.pad(v, ((0, 0), (0, pad), (0, 0), (0, 0)))
    slopes = alibi_slopes()[:, None, None, None]
    pos_k = jnp.arange(lp)
    scale = QK_DIM ** -0.5
    qb = qp.reshape(b, n_blk, Q_BLOCK, N_HEADS, 2, QK_DIM).transpose(1, 0, 2, 3, 4, 5)

    def one_block(args):
        q_blk, blk = args
        pos_q = blk * Q_BLOCK + jnp.arange(Q_BLOCK)
        dist = (pos_q[:, None] - pos_k[None, :]).astype(jnp.float32)
        s = jnp.einsum('bqhmd,bkhmd->bhmqk', q_blk, kp, preferred_element_type=jnp.float32) * scale
        s = jnp.where(dist >= 0, s - slopes * dist, NEG_INF)
        p = jax.nn.softmax(s, axis=-1)
        a = (p[:, :, 0] - lam * p[:, :, 1]).astype(vp.dtype)
        return jnp.einsum('bhqk,bkhd->bqhd', a, vp)

    o = lax.map(one_block, (qb, jnp.arange(n_blk)))
    return o.transpose(1, 0, 2, 3, 4).reshape(b, lp, N_HEADS, V_DIM)[:, :L]


def diff_attn_sample(q, k, v, lam, k_past, v_past):
    P, T = k_past.shape[1], q.shape[1]
    slopes = alibi_slopes()[:, None, None, None]
    scale = QK_DIM ** -0.5
    pos_q = P + jnp.arange(T)
    d_past = (pos_q[:, None] - jnp.arange(P)[None, :]).astype(jnp.float32)
    d_new = (jnp.arange(T)[:, None] - jnp.arange(T)[None, :]).astype(jnp.float32)
    s_past = jnp.einsum('bqhmd,bkhmd->bhmqk', q, k_past, preferred_element_type=jnp.float32) * scale - slopes * d_past
    s_new = jnp.einsum('bqhmd,bkhmd->bhmqk', q, k, preferred_element_type=jnp.float32) * scale
    s_new = jnp.where(d_new >= 0, s_new - slopes * d_new, NEG_INF)
    p = jax.nn.softmax(jnp.concatenate([s_past, s_new], axis=-1), axis=-1)
    a = (p[:, :, 0] - lam * p[:, :, 1]).astype(v.dtype)
    return (jnp.einsum('bhqk,bkhd->bqhd', a[..., :P], v_past)
            + jnp.einsum('bhqk,bkhd->bqhd', a[..., P:], v))


def pool_mix(hist, z, pos0, w_lin, scale):
    b, t = z.shape[0], z.shape[1]
    full = jnp.concatenate([hist, z], axis=1)
    cs = jnp.pad(jnp.cumsum(full.astype(jnp.float32), axis=1), ((0, 0), (1, 0), (0, 0)))
    pos = (pos0 + jnp.arange(t)).astype(jnp.float32)
    zf = z.astype(jnp.float32)
    outs = []
    for g, w in enumerate(POOL_WINDOWS):
        c0, c1 = g * POOL_GROUP, (g + 1) * POOL_GROUP
        hi = cs[:, POOL_HIST + 1:POOL_HIST + 1 + t, c0:c1]
        lo = cs[:, POOL_HIST + 1 - w:POOL_HIST + 1 - w + t, c0:c1]
        cnt = jnp.minimum(pos + 1.0, float(w))
        outs.append((hi - lo) / cnt[None, :, None] - zf[..., c0:c1])
    pooled = jnp.stack(outs, axis=2)
    y = jnp.einsum('btgc,gcd->btgd', pooled, w_lin.astype(jnp.float32)).reshape(b, t, POOL_WIDTH)
    y = y * scale.astype(jnp.float32)
    return y.astype(z.dtype), full[:, -POOL_HIST:]


def short_conv(hist, u, conv_w, gate_b):
    t = u.shape[1]
    full = jnp.concatenate([hist, u], axis=1)
    y = sum(conv_w[j] * full[:, j:j + t] for j in range(CONV_K))
    return gate_b * y, full[:, -(CONV_K - 1):]


def decoder_layer(x, attend, pool_hist, conv_hist, pos0, lam, lam_init, w_in, w_o, g_subln,
                  w_pool, pool_scale, conv_w, w_gate_up, w_down,
                  g_pre_mix, g_post_mix, g_pre_ffn, g_post_ffn):
    b, t, _ = x.shape
    h = rmsnorm(x, g_pre_mix)
    proj = h @ w_in
    o1 = QK_COLS
    o2 = 2 * QK_COLS
    o3 = o2 + V_COLS
    o4 = o3 + POOL_WIDTH
    o5 = o4 + CONV_CH
    o6 = o5 + CONV_CH
    q = proj[..., :o1].reshape(b, t, N_HEADS, 2, QK_DIM)
    k = proj[..., o1:o2].reshape(b, t, N_HEADS, 2, QK_DIM)
    v = proj[..., o2:o3].reshape(b, t, N_HEADS, V_DIM)
    zp = proj[..., o3:o4]
    u = proj[..., o4:o5]
    gate_b = proj[..., o5:o6]
    gate_c = proj[..., o6:]
    o_attn = attend(q, k, v, lam)
    o_attn = (rmsnorm(o_attn, g_subln) * (1.0 - lam_init)).reshape(b, t, ATTN_WIDTH)
    o_pool, pool_state = pool_mix(pool_hist, zp, pos0, w_pool, pool_scale)
    o_conv, conv_state = short_conv(conv_hist, gate_c * u, conv_w, gate_b)
    mix = jnp.concatenate([o_attn, o_pool, o_conv], axis=-1) @ w_o
    x = x + rmsnorm(mix, g_post_mix)
    h = rmsnorm(x, g_pre_ffn)
    gu = h @ w_gate_up
    f = (jax.nn.silu(gu[..., :FFN_HIDDEN]) * gu[..., FFN_HIDDEN:]) @ w_down
    x = x + rmsnorm(f, g_post_ffn)
    return x, k, v, pool_state, conv_state


def setup_inputs(seed: int = 0) -> dict:
    key = jax.random.key(seed)
    ks = jax.random.split(key, 24)
    f32 = jnp.float32
    n_pages = PAST_LEN // PAGE_SIZE
    n_used = DEC_BATCH * n_pages
    n_pool_pages = n_used + (n_used + 3) // 4

    def nrm(k, shape, s):
        return jax.random.normal(k, shape, f32) * s

    page_table = jax.random.permutation(ks[0], n_pool_pages)[:n_used].reshape(DEC_BATCH, n_pages).astype(jnp.int32)
    return {
        'x_prompt': nrm(ks[1], (BATCH, SEQ, D_MODEL), 1.0),
        'x_sample': nrm(ks[2], (DEC_BATCH, DEC_SEQ, D_MODEL), 1.0),
        'cache_k': nrm(ks[3], (DEPTH, n_pool_pages, PAGE_SIZE, N_HEADS, 2, QK_DIM), 1.0),
        'cache_v': nrm(ks[4], (DEPTH, n_pool_pages, PAGE_SIZE, N_HEADS, V_DIM), 1.0),
        'state_pool': nrm(ks[5], (DEPTH, DEC_BATCH, POOL_HIST, POOL_WIDTH), 1.0),
        'state_conv': nrm(ks[6], (DEPTH, DEC_BATCH, CONV_K - 1, CONV_CH), 1.0),
        'page_table': page_table,
        'meta_tokens': nrm(ks[7], (N_META, D_MODEL), 1.0),
        'w_in': nrm(ks[8], (DEPTH, D_MODEL, PROJ_WIDTH), D_MODEL ** -0.5),
        'w_o': nrm(ks[9], (DEPTH, MIX_WIDTH, D_MODEL), MIX_WIDTH ** -0.5),
        'lambda_q1': nrm(ks[10], (DEPTH, QK_DIM), 0.1),
        'lambda_k1': nrm(ks[11], (DEPTH, QK_DIM), 0.1),
        'lambda_q2': nrm(ks[12], (DEPTH, QK_DIM), 0.1),
        'lambda_k2': nrm(ks[13], (DEPTH, QK_DIM), 0.1),
        'g_subln': 1.0 + nrm(ks[14], (DEPTH, V_DIM), 0.02),
        'w_pool': nrm(ks[15], (DEPTH, len(POOL_WINDOWS), POOL_GROUP, POOL_GROUP), POOL_GROUP ** -0.5),
        'pool_scale': 1.0 + nrm(ks[16], (DEPTH, POOL_WIDTH), 0.02),
        'conv_w': nrm(ks[17], (DEPTH, CONV_K, CONV_CH), CONV_K ** -0.5),
        'w_gate_up': nrm(ks[18], (DEPTH, D_MODEL, 2 * FFN_HIDDEN), D_MODEL ** -0.5),
        'w_down': nrm(ks[19], (DEPTH, FFN_HIDDEN, D_MODEL), FFN_HIDDEN ** -0.5),
        'g_pre_mix': 1.0 + nrm(ks[20], (DEPTH, D_MODEL), 0.02),
        'g_post_mix': 1.0 + nrm(ks[21], (DEPTH, D_MODEL), 0.02),
        'g_pre_ffn': 1.0 + nrm(ks[22], (DEPTH, D_MODEL), 0.02),
        'g_post_ffn': 1.0 + nrm(ks[23], (DEPTH, D_MODEL), 0.02),
    }


def reference(x_prompt, x_sample, cache_k, cache_v, state_pool, state_conv, page_table, meta_tokens,
              w_in, w_o, lambda_q1, lambda_k1, lambda_q2, lambda_k2, g_subln, w_pool, pool_scale,
              conv_w, w_gate_up, w_down, g_pre_mix, g_post_mix, g_pre_ffn, g_post_ffn):
    b = x_prompt.shape[0]
    bd = x_sample.shape[0]
    n_pages = page_table.shape[1]
    past_len = n_pages * cache_k.shape[2]
    meta = jnp.broadcast_to(meta_tokens[None].astype(x_prompt.dtype), (b, N_META, D_MODEL))
    xp = jnp.concatenate([meta, x_prompt], axis=1)
    xs = x_sample
    pool_hist_p = jnp.zeros((b, POOL_HIST, POOL_WIDTH), xp.dtype)
    conv_hist_p = jnp.zeros((b, CONV_K - 1, CONV_CH), xp.dtype)
    kp_l, vp_l, pp_l, cp_l = [], [], [], []
    ks_l, vs_l, ps_l, cs_l = [], [], [], []
    for l in range(DEPTH):
        lam_init = 0.8 - 0.6 * math.exp(-0.3 * l)
        lam = (jnp.exp(jnp.sum(lambda_q1[l].astype(jnp.float32) * lambda_k1[l].astype(jnp.float32)))
               - jnp.exp(jnp.sum(lambda_q2[l].astype(jnp.float32) * lambda_k2[l].astype(jnp.float32)))
               + lam_init)
        lw = (w_in[l], w_o[l], g_subln[l], w_pool[l], pool_scale[l], conv_w[l], w_gate_up[l], w_down[l],
              g_pre_mix[l], g_post_mix[l], g_pre_ffn[l], g_post_ffn[l])
        xp, k_new, v_new, p_new, c_new = decoder_layer(
            xp, diff_attn_prompt, pool_hist_p, conv_hist_p, 0, lam, lam_init, *lw)
        kp_l.append(k_new)
        vp_l.append(v_new)
        pp_l.append(p_new)
        cp_l.append(c_new)
        k_past = cache_k[l][page_table].reshape(bd, past_len, N_HEADS, 2, QK_DIM)
        v_past = cache_v[l][page_table].reshape(bd, past_len, N_HEADS, V_DIM)
        attend_s = functools.partial(diff_attn_sample, k_past=k_past, v_past=v_past)
        xs, k_new, v_new, p_new, c_new = decoder_layer(
            xs, attend_s, state_pool[l], state_conv[l], past_len, lam, lam_init, *lw)
        ks_l.append(k_new)
        vs_l.append(v_new)
        ps_l.append(p_new)
        cs_l.append(c_new)
    y_prompt = xp[:, N_META:]
    y_sample = xs
    return (y_prompt, y_sample,
            jnp.stack(kp_l), jnp.stack(vp_l), jnp.stack(pp_l), jnp.stack(cp_l),
            jnp.stack(ks_l), jnp.stack(vs_l), jnp.stack(ps_l), jnp.stack(cs_l))
```

```python
import functools
import math

import jax
import jax.numpy as jnp
from jax import lax
from jax.experimental import pallas as pl
from jax.experimental.pallas import tpu as pltpu

N_META = 16
N_HEADS = 8
QK_DIM = 64
V_DIM = 128
HEAD_COLS = 2 * QK_DIM
ATTN_W = N_HEADS * V_DIM
POOL_W = 512
POOL_WINDOWS = (2, 4, 8, 16)
POOL_GROUP = 128
CONV_CH = 512
HIST = 16
EPS = 1e-6
NEG_INF = -1e30
PAGE = 128

F32 = jnp.float32
BF16 = jnp.bfloat16

VMEM_LIMIT_BYTES = 56 * 1024 * 1024


def _params(sem):
    return pltpu.CompilerParams(dimension_semantics=sem, vmem_limit_bytes=VMEM_LIMIT_BYTES)


def _resident(shape, index_map):
    return pl.BlockSpec(shape, index_map, pipeline_mode=pl.Buffered(1))


def _rms(x, g):
    r = lax.rsqrt(jnp.mean(x * x, axis=-1, keepdims=True) + EPS)
    return x * r * g


def _inproj_kernel(x_ref, g_ref, w_ref, q_ref, ktf_ref, ktb_ref, vf_ref, vb_ref, zpu_ref, gat_ref):
    h = _rms(x_ref[...], g_ref[...]).astype(BF16)

    def proj(c):
        return jnp.dot(h, w_ref[:, c * ATTN_W:(c + 1) * ATTN_W], preferred_element_type=F32)

    q_ref[...] = (proj(0) * (QK_DIM ** -0.5)).astype(BF16)
    kt = proj(1).T
    ktf_ref[0] = kt
    ktb_ref[0, 0] = kt.astype(BF16)
    v = proj(2)
    vf_ref[...] = v
    vb_ref[...] = v.astype(BF16)
    zpu_ref[...] = proj(3)
    gat_ref[...] = proj(4)


def _inproj(x, g, w, tm, rows_per_seq):
    n, d = x.shape
    assert n % tm == 0 and rows_per_seq % tm == 0 and n % rows_per_seq == 0
    n_seq = n // rows_per_seq
    tps = rows_per_seq // tm
    row = lambda i: (i, 0)
    out_shape = (
        jax.ShapeDtypeStruct((n, ATTN_W), BF16),
        jax.ShapeDtypeStruct((n_seq, ATTN_W, rows_per_seq), F32),
        jax.ShapeDtypeStruct((n_seq, tps, ATTN_W, tm), BF16),
        jax.ShapeDtypeStruct((n, ATTN_W), F32),
        jax.ShapeDtypeStruct((n, ATTN_W), BF16),
        jax.ShapeDtypeStruct((n, ATTN_W), F32),
        jax.ShapeDtypeStruct((n, ATTN_W), F32),
    )
    out_specs = (
        pl.BlockSpec((tm, ATTN_W), row),
        pl.BlockSpec((1, ATTN_W, tm), lambda i: (i // tps, 0, i % tps)),
        pl.BlockSpec((1, 1, ATTN_W, tm), lambda i: (i // tps, i % tps, 0, 0)),
        pl.BlockSpec((tm, ATTN_W), row),
        pl.BlockSpec((tm, ATTN_W), row),
        pl.BlockSpec((tm, ATTN_W), row),
        pl.BlockSpec((tm, ATTN_W), row),
    )
    return pl.pallas_call(
        _inproj_kernel,
        grid=(n // tm,),
        in_specs=[
            pl.BlockSpec((tm, d), row),
            _resident((1, d), lambda i: (0, 0)),
            _resident(w.shape, lambda i: (0, 0)),
        ],
        out_specs=out_specs,
        out_shape=out_shape,
        compiler_params=_params(("parallel",)),
        name="inproj",
    )(x, g, w)


def _lambda(lamp_ref, lam_init):
    lp = lamp_ref[...]
    s1 = jnp.sum(lp[0:1] * lp[1:2], axis=-1, keepdims=True)
    s2 = jnp.sum(lp[2:3] * lp[3:4], axis=-1, keepdims=True)
    return jnp.exp(s1) - jnp.exp(s2) + lam_init


def _slope_of_head(h):
    bits = lax.shift_left(jnp.full((1, 1), 126, jnp.int32) - h, 23)
    return lax.bitcast_convert_type(bits, F32)


def _softmax_update(z, v_blk, m_ref, l_ref, acc_ref):
    m_old = m_ref[...]
    m_new = jnp.maximum(m_old, jnp.max(z, axis=-1, keepdims=True))
    alpha = jnp.exp(m_old - m_new)
    p = jnp.exp(z - m_new)
    l_ref[...] = alpha * l_ref[...] + jnp.sum(p, axis=-1, keepdims=True)
    acc_ref[...] = alpha * acc_ref[...] + jnp.dot(p.astype(BF16), v_blk, preferred_element_type=F32)
    m_ref[...] = m_new


def _attn_kernel(*refs, tq, n_main, pos_base0, lam_init):
    if n_main:
        q_ref, ktm_ref, vm_ref, kt_ref, v_ref, lamp_ref, g_ref, o_ref, m_ref, l_ref, acc_ref = refs
    else:
        q_ref, ktm_ref, vm_ref, lamp_ref, g_ref, o_ref, m_ref, l_ref, acc_ref = refs
    h = pl.program_id(1)
    qi = pl.program_id(2)
    slope = _slope_of_head(h)

    q = q_ref[...]
    lane = lax.broadcasted_iota(jnp.int32, q.shape, 1)
    zero = jnp.zeros_like(q)
    q2 = jnp.concatenate([jnp.where(lane < QK_DIM, q, zero), jnp.where(lane >= QK_DIM, q, zero)], axis=0)

    m_ref[...] = jnp.full(m_ref.shape, NEG_INF, F32)
    l_ref[...] = jnp.zeros(l_ref.shape, F32)
    acc_ref[...] = jnp.zeros(acc_ref.shape, F32)

    def local_row(shape):
        r = lax.broadcasted_iota(jnp.int32, shape, 0)
        return jnp.where(r >= tq, r - tq, r)

    shape_m = (2 * tq, PAGE)
    qpos = local_row(shape_m) + (pos_base0 + qi * tq)
    kj = lax.broadcasted_iota(jnp.int32, shape_m, 1)
    s = jnp.dot(q2, ktm_ref[0], preferred_element_type=F32)
    dist = (qpos - kj).astype(F32)
    z = jnp.where((kj < N_META) & (kj <= qpos), s - slope * dist, NEG_INF)
    _softmax_update(z, vm_ref[0], m_ref, l_ref, acc_ref)

    if n_main:
        shape = (2 * tq, tq)
        rel = (local_row(shape) - lax.broadcasted_iota(jnp.int32, shape, 1)).astype(F32)
        nb = -slope * rel

        def full_block(kb, carry):
            s = jnp.dot(q2, kt_ref[0, kb], preferred_element_type=F32)
            off = ((qi - kb) * tq).astype(F32)
            z = s + (nb - slope * off)
            start = pl.multiple_of(kb * tq, tq)
            _softmax_update(z, v_ref[pl.ds(start, tq), :], m_ref, l_ref, acc_ref)
            return carry

        lax.fori_loop(0, qi, full_block, 0)

        s = jnp.dot(q2, kt_ref[0, qi], preferred_element_type=F32)
        z = jnp.where(rel >= 0, s + nb, NEG_INF)
        start = pl.multiple_of(qi * tq, tq)
        _softmax_update(z, v_ref[pl.ds(start, tq), :], m_ref, l_ref, acc_ref)

    lam = _lambda(lamp_ref, lam_init)
    acc = acc_ref[...]
    l = l_ref[...]
    o = acc[:tq] / l[:tq] - lam * (acc[tq:] / l[tq:])
    o_ref[...] = (_rms(o, g_ref[...]) * (1.0 - lam_init)).astype(BF16)


def _attention(q, kt_meta, v_meta, kt_main, v_main, lamp, g_subln, *, n_seq, rows_per_seq, tq,
               pos_base0, lam_init):
    nq = rows_per_seq // tq
    n_main = 0 if kt_main is None else nq
    in_specs = [
        pl.BlockSpec((tq, HEAD_COLS), lambda b, h, i: (b * nq + i, h)),
        pl.BlockSpec((1, HEAD_COLS, PAGE), lambda b, h, i: (b, h, 0)),
        pl.BlockSpec((1, PAGE, V_DIM), lambda b, h, i: (b, 0, h)),
    ]
    args = [q, kt_meta, v_meta]
    if n_main:
        in_specs += [
            pl.BlockSpec((1, nq, HEAD_COLS, tq), lambda b, h, i: (b, 0, h, 0)),
            pl.BlockSpec((rows_per_seq, V_DIM), lambda b, h, i: (b, h)),
        ]
        args += [kt_main, v_main]
    in_specs += [
        pl.BlockSpec((4, QK_DIM), lambda b, h, i: (0, 0)),
        pl.BlockSpec((1, V_DIM), lambda b, h, i: (0, 0)),
    ]
    args += [lamp, g_subln]
    return pl.pallas_call(
        functools.partial(_attn_kernel, tq=tq, n_main=n_main, pos_base0=pos_base0, lam_init=lam_init),
        grid=(n_seq, N_HEADS, nq),
        in_specs=in_specs,
        out_specs=pl.BlockSpec((tq, V_DIM), lambda b, h, i: (b * nq + i, h)),
        out_shape=jax.ShapeDtypeStruct((n_seq * rows_per_seq, ATTN_W), BF16),
        scratch_shapes=[
            pltpu.VMEM((2 * tq, 1), F32),
            pltpu.VMEM((2 * tq, 1), F32),
            pltpu.VMEM((2 * tq, V_DIM), F32),
        ],
        compiler_params=_params(("parallel", "parallel", "parallel")),
        name="attn_prompt" if n_main else "attn_meta",
    )(*args)


def _sample_attn_kernel(pt_ref, q_ref, ktn_ref, vn_ref, *rest, pp, t_dec, past_len, lam_init):
    kt_refs = rest[:pp]
    v_refs = rest[pp:2 * pp]
    lamp_ref, g_ref, o_ref, qbd_ref, m_ref, l_ref, acc_ref = rest[2 * pp:]
    s_idx = pl.program_id(1)
    rows = N_HEADS * 2 * t_dec
    grp = 2 * t_dec

    r_col = lax.broadcasted_iota(jnp.int32, (rows, 1), 0)
    head = r_col // grp
    t_of_r = (r_col % t_dec).astype(F32)
    slope = lax.bitcast_convert_type((126 - head) << 23, F32)

    def pv_update(z, v_of_head):
        m_old = m_ref[...]
        m_new = jnp.maximum(m_old, jnp.max(z, axis=-1, keepdims=True))
        alpha = jnp.exp(m_old - m_new)
        p = jnp.exp(z - m_new)
        l_ref[...] = alpha * l_ref[...] + jnp.sum(p, axis=-1, keepdims=True)
        pb = p.astype(BF16)
        pv = [jnp.dot(pb[hh * grp:(hh + 1) * grp], v_of_head(hh), preferred_element_type=F32)
              for hh in range(N_HEADS)]
        acc_ref[...] = alpha * acc_ref[...] + jnp.concatenate(pv, axis=0)
        m_ref[...] = m_new

    @pl.when(s_idx == 0)
    def _init():
        qf = q_ref[0].astype(F32)
        qt = jnp.concatenate([qf] * (2 * N_HEADS), axis=0)
        rr = lax.broadcasted_iota(jnp.int32, qt.shape, 0) // t_dec
        cc = lax.broadcasted_iota(jnp.int32, qt.shape, 1) // QK_DIM
        qbd_ref[...] = jnp.where(rr == cc, qt, 0.0).astype(BF16)
        m_ref[...] = jnp.full(m_ref.shape, NEG_INF, F32)
        l_ref[...] = jnp.zeros(l_ref.shape, F32)
        acc_ref[...] = jnp.zeros(acc_ref.shape, F32)
        s = jnp.dot(qbd_ref[...], ktn_ref[0], preferred_element_type=F32)
        kj = lax.broadcasted_iota(jnp.int32, s.shape, 1)
        tq_i = lax.broadcasted_iota(jnp.int32, s.shape, 0) % t_dec
        dist = (tq_i - kj).astype(F32)
        z = jnp.where((kj < t_dec) & (kj <= tq_i), s - slope * dist, NEG_INF)
        vn = vn_ref[0]
        pv_update(z, lambda hh: vn[:, hh * V_DIM:(hh + 1) * V_DIM])

    qbd = qbd_ref[...]
    s = jnp.concatenate(
        [jnp.dot(qbd, kt_refs[j][0, 0].astype(BF16), preferred_element_type=F32) for j in range(pp)],
        axis=-1)
    col = lax.broadcasted_iota(jnp.int32, s.shape, 1).astype(F32)
    base = (s_idx * (pp * PAGE)).astype(F32)
    dist = (past_len + t_of_r - base) - col
    z = s - slope * dist

    def v_cached(hh):
        return jnp.concatenate(
            [v_refs[j][0, 0, pl.ds(hh, PAGE, stride=N_HEADS), :].astype(BF16) for j in range(pp)], axis=0)

    pv_update(z, v_cached)

    @pl.when(s_idx == pl.num_programs(1) - 1)
    def _fin():
        lam = _lambda(lamp_ref, lam_init)
        acc = acc_ref[...]
        l = l_ref[...]
        g = g_ref[...]
        outs = []
        for hh in range(N_HEADS):
            a0, a1 = hh * grp, hh * grp + t_dec
            o = acc[a0:a0 + t_dec] / l[a0:a0 + t_dec] - lam * (acc[a1:a1 + t_dec] / l[a1:a1 + t_dec])
            outs.append(_rms(o, g) * (1.0 - lam_init))
        o_ref[0] = jnp.concatenate(outs, axis=-1).astype(BF16)


def _sample_attention(page_table, q_s, kt_new, v_new, kt_cache, v_cache, lamp, g_subln, *, layer, pp,
                      lam_init):
    bd, t_dec, _ = q_s.shape
    n_pages = page_table.shape[1]
    assert n_pages % pp == 0
    rows = N_HEADS * 2 * t_dec

    def page_spec(j):
        return pl.BlockSpec((1, 1, ATTN_W, PAGE), lambda b, s, pt: (layer, pt[b, s * pp + j], 0, 0))

    in_specs = [
        pl.BlockSpec((1, t_dec, ATTN_W), lambda b, s, pt: (b, 0, 0)),
        pl.BlockSpec((1, ATTN_W, PAGE), lambda b, s, pt: (b, 0, 0)),
        pl.BlockSpec((1, PAGE, ATTN_W), lambda b, s, pt: (b, 0, 0)),
    ]
    in_specs += [page_spec(j) for j in range(pp)] + [page_spec(j) for j in range(pp)]
    in_specs += [
        pl.BlockSpec((4, QK_DIM), lambda b, s, pt: (0, 0)),
        pl.BlockSpec((1, V_DIM), lambda b, s, pt: (0, 0)),
    ]
    grid_spec = pltpu.PrefetchScalarGridSpec(
        num_scalar_prefetch=1,
        grid=(bd, n_pages // pp),
        in_specs=in_specs,
        out_specs=pl.BlockSpec((1, t_dec, ATTN_W), lambda b, s, pt: (b, 0, 0)),
        scratch_shapes=[
            pltpu.VMEM((rows, ATTN_W), BF16),
            pltpu.VMEM((rows, 1), F32),
            pltpu.VMEM((rows, 1), F32),
            pltpu.VMEM((rows, V_DIM), F32),
        ],
    )
    return pl.pallas_call(
        functools.partial(_sample_attn_kernel, pp=pp, t_dec=t_dec, past_len=float(n_pages * PAGE),
                          lam_init=lam_init),
        grid_spec=grid_spec,
        out_shape=jax.ShapeDtypeStruct((bd, t_dec, ATTN_W), BF16),
        compiler_params=_params(("parallel", "arbitrary")),
        name="attn_sample",
    )(page_table, q_s, kt_new, v_new, *([kt_cache] * pp), *([v_cache] * pp), lamp, g_subln)


def _seq_mixers(zf_ref, ucf_ref, t, z, u, gate_b, gate_c, wpool_ref, pscale, cw, ramp_cnt):
    zf_ref[HIST:HIST + t] = z
    uc = gate_c * u
    ucf_ref[HIST:HIST + t] = uc
    pooled_out = []
    for gi, w in enumerate(POOL_WINDOWS):
        cols = pl.ds(gi * POOL_GROUP, POOL_GROUP)
        zg = z[:, gi * POOL_GROUP:(gi + 1) * POOL_GROUP]
        acc = zg
        for j in range(1, w):
            acc = acc + zf_ref[pl.ds(HIST - j, t), cols]
        if ramp_cnt:
            pos = lax.broadcasted_iota(jnp.int32, (t, 1), 0)
            cnt = jnp.minimum(pos + 1, w).astype(F32)
            pooled = acc / cnt - zg
        else:
            pooled = acc * (1.0 / w) - zg
        pooled_out.append(jnp.dot(pooled.astype(BF16), wpool_ref[gi], preferred_element_type=F32))
    o_pool = jnp.concatenate(pooled_out, axis=-1) * pscale
    y = cw[0:1] * ucf_ref[pl.ds(HIST - 2, t), :] + cw[1:2] * ucf_ref[pl.ds(HIST - 1, t), :] + cw[2:3] * uc
    return o_pool, gate_b * y, uc


def _mix_project(x, o_attn, o_pool, o_conv, wo_ref, g_post):
    mix = jnp.dot(o_attn, wo_ref[0:ATTN_W], preferred_element_type=F32)
    mix += jnp.dot(o_pool.astype(BF16), wo_ref[ATTN_W:ATTN_W + POOL_W], preferred_element_type=F32)
    mix += jnp.dot(o_conv.astype(BF16), wo_ref[ATTN_W + POOL_W:], preferred_element_type=F32)
    return x + _rms(mix, g_post)


def _mix_main_kernel(x_ref, oa_ref, zpu_ref, gat_ref, hz_ref, hg_ref, wo_ref, wpool_ref, ps_ref, cw_ref,
                     g_ref, xo_ref, uct_ref, zf_ref, ucf_ref, *, tm, tiles_per_seq):
    i = pl.program_id(0)

    @pl.when(i % tiles_per_seq == 0)
    def _():
        hz = hz_ref[0]
        hg = hg_ref[0]
        zf_ref[0:HIST] = hz[:, :POOL_W]
        ucf_ref[0:HIST] = hg[:, CONV_CH:] * hz[:, POOL_W:]

    @pl.when(i % tiles_per_seq != 0)
    def _():
        zf_ref[0:HIST] = zf_ref[tm:tm + HIST]
        ucf_ref[0:HIST] = ucf_ref[tm:tm + HIST]

    zpu = zpu_ref[...]
    gat = gat_ref[...]
    o_pool, o_conv, uc = _seq_mixers(zf_ref, ucf_ref, tm, zpu[:, :POOL_W], zpu[:, POOL_W:], gat[:, :CONV_CH],
                                     gat[:, CONV_CH:], wpool_ref, ps_ref[...], cw_ref[...], False)
    uct_ref[0] = uc[tm - HIST:]
    xo_ref[...] = _mix_project(x_ref[...], oa_ref[...], o_pool, o_conv, wo_ref, g_ref[...])


def _mix_main(x, o_attn, zpu, gates, hist_zpu, hist_gates, wo, wpool, pscale, cw, g_post, *, tm, rows_per_seq):
    n, d = x.shape
    tps = rows_per_seq // tm
    n_seq = n // rows_per_seq
    row = lambda i: (i, 0)
    const2 = lambda i: (0, 0)
    return pl.pallas_call(
        functools.partial(_mix_main_kernel, tm=tm, tiles_per_seq=tps),
        grid=(n // tm,),
        in_specs=[
            pl.BlockSpec((tm, d), row),
            pl.BlockSpec((tm, ATTN_W), row),
            pl.BlockSpec((tm, ATTN_W), row),
            pl.BlockSpec((tm, ATTN_W), row),
            pl.BlockSpec((1, HIST, ATTN_W), lambda i: (i // tps, 0, 0)),
            pl.BlockSpec((1, HIST, ATTN_W), lambda i: (i // tps, 0, 0)),
            _resident(wo.shape, const2),
            _resident(wpool.shape, lambda i: (0, 0, 0)),
            _resident(pscale.shape, const2),
            _resident(cw.shape, const2),
            _resident(g_post.shape, const2),
        ],
        out_specs=(
            pl.BlockSpec((tm, d), row),
            pl.BlockSpec((1, HIST, CONV_CH), lambda i: (i // tps, 0, 0)),
        ),
        out_shape=(
            jax.ShapeDtypeStruct((n, d), F32),
            jax.ShapeDtypeStruct((n_seq, HIST, CONV_CH), F32),
        ),
        scratch_shapes=[pltpu.VMEM((HIST + tm, POOL_W), F32), pltpu.VMEM((HIST + tm, CONV_CH), F32)],
        compiler_params=_params(("arbitrary",)),
        name="mix_main",
    )(x, o_attn, zpu, gates, hist_zpu, hist_gates, wo, wpool, pscale, cw, g_post)


def _mix_small_kernel(x_ref, oa_ref, zpu_ref, gat_ref, hp_ref, hc_ref, wo_ref, wpool_ref, ps_ref, cw_ref,
                      g_ref, xo_ref, uc_ref, zf_ref, ucf_ref, *, n_meta_seq, n_dec, t_dec):
    pools, convs, ucs = [], [], []
    segs = [(s * N_META, N_META, None) for s in range(n_meta_seq)]
    segs += [(n_meta_seq * N_META + b * t_dec, t_dec, b) for b in range(n_dec)]
    for start, t, b in segs:
        if b is None:
            zf_ref[0:HIST] = jnp.zeros((HIST, POOL_W), F32)
            ucf_ref[0:HIST] = jnp.zeros((HIST, CONV_CH), F32)
        else:
            zf_ref[0:HIST] = hp_ref[b]
            ucf_ref[0:HIST] = hc_ref[b]
        zpu = zpu_ref[start:start + t]
        gat = gat_ref[start:start + t]
        o_pool, o_conv, uc = _seq_mixers(zf_ref, ucf_ref, t, zpu[:, :POOL_W], zpu[:, POOL_W:], gat[:, :CONV_CH],
                                         gat[:, CONV_CH:], wpool_ref, ps_ref[...], cw_ref[...], b is None)
        pools.append(o_pool)
        convs.append(o_conv)
        ucs.append(uc)
    n_pad = x_ref.shape[0] - (n_meta_seq * N_META + n_dec * t_dec)
    if n_pad:
        pad = jnp.zeros((n_pad, POOL_W), F32)
        pools.append(pad)
        convs.append(pad)
        ucs.append(pad)
    uc_ref[...] = jnp.concatenate(ucs, axis=0)
    xo_ref[...] = _mix_project(x_ref[...], oa_ref[...], jnp.concatenate(pools, axis=0),
                               jnp.concatenate(convs, axis=0), wo_ref, g_ref[...])


def _mix_small(x, o_attn, zpu, gates, hist_pool, hist_conv, wo, wpool, pscale, cw, g_post, *, n_meta_seq,
               n_dec, t_dec):
    n, d = x.shape
    return pl.pallas_call(
        functools.partial(_mix_small_kernel, n_meta_seq=n_meta_seq, n_dec=n_dec, t_dec=t_dec),
        out_shape=(jax.ShapeDtypeStruct((n, d), F32), jax.ShapeDtypeStruct((n, CONV_CH), F32)),
        scratch_shapes=[pltpu.VMEM((HIST + N_META, POOL_W), F32), pltpu.VMEM((HIST + N_META, CONV_CH), F32)],
        compiler_params=pltpu.CompilerParams(vmem_limit_bytes=VMEM_LIMIT_BYTES),
        name="mix_small",
    )(x, o_attn, zpu, gates, hist_pool, hist_conv, wo, wpool, pscale, cw, g_post)


def _ffn_kernel(x_ref, gpre_ref, wg_ref, wu_ref, wd_ref, gpost_ref, o_ref, h_ref, acc_ref):
    c = pl.program_id(1)

    @pl.when(c == 0)
    def _():
        h_ref[...] = _rms(x_ref[...], gpre_ref[...]).astype(BF16)
        acc_ref[...] = jnp.zeros(acc_ref.shape, F32)

    h = h_ref[...]
    gate = jnp.dot(h, wg_ref[...], preferred_element_type=F32)
    up = jnp.dot(h, wu_ref[...], preferred_element_type=F32)
    act = (gate * jax.nn.sigmoid(gate) * up).astype(BF16)
    acc_ref[...] += jnp.dot(act, wd_ref[...], preferred_element_type=F32)

    @pl.when(c == pl.num_programs(1) - 1)
    def _():
        o_ref[...] = x_ref[...] + _rms(acc_ref[...], gpost_ref[...])


def _ffn(x, g_pre, w_gate_up, w_down, g_post, *, tm, tc):
    n, d = x.shape
    hidden = w_down.shape[0]
    assert n % tm == 0 and hidden % tc == 0
    nc = hidden // tc
    return pl.pallas_call(
        _ffn_kernel,
        grid=(n // tm, nc),
        in_specs=[
            pl.BlockSpec((tm, d), lambda i, c: (i, 0)),
            pl.BlockSpec((1, d), lambda i, c: (0, 0)),
            pl.BlockSpec((d, tc), lambda i, c: (0, c)),
            pl.BlockSpec((d, tc), lambda i, c: (0, nc + c)),
            pl.BlockSpec((tc, d), lambda i, c: (c, 0)),
            pl.BlockSpec((1, d), lambda i, c: (0, 0)),
        ],
        out_specs=pl.BlockSpec((tm, d), lambda i, c: (i, 0)),
        out_shape=jax.ShapeDtypeStruct((n, d), F32),
        scratch_shapes=[pltpu.VMEM((tm, d), BF16), pltpu.VMEM((tm, d), F32)],
        compiler_params=_params(("parallel", "arbitrary")),
        name="ffn",
    )(x, g_pre, w_gate_up, w_gate_up, w_down, g_post)


TM_INPROJ = 256
TQ_ATTN = 256
TM_MIX = 512
TM_FFN = 512
TC_FFN = 512
PAGES_PER_STEP = 4


def kernel(x_prompt, x_sample, cache_k, cache_v, state_pool, state_conv, page_table, meta_tokens, w_in, w_o,
           lambda_q1, lambda_k1, lambda_q2, lambda_k2, g_subln, w_pool, pool_scale, conv_w, w_gate_up, w_down,
           g_pre_mix, g_post_mix, g_pre_ffn, g_post_ffn):
    nb, seq, d = x_prompt.shape
    bd, t_dec, _ = x_sample.shape
    depth = w_in.shape[0]
    n_pool = cache_k.shape[1]
    n_small = nb * N_META + bd * t_dec
    meta_rows = nb * N_META
    ns_pad = -(-n_small // PAGE) * PAGE

    kt_cache = jnp.transpose(cache_k, (0, 1, 3, 4, 5, 2)).reshape(depth, n_pool, ATTN_W, PAGE)
    v_cache = cache_v.reshape(depth, n_pool, PAGE * N_HEADS, V_DIM)

    xm = x_prompt.reshape(nb * seq, d)
    xs = jnp.concatenate([jnp.broadcast_to(meta_tokens[None], (nb, N_META, d)).reshape(meta_rows, d),
                          x_sample.reshape(bd * t_dec, d), jnp.zeros((ns_pad - n_small, d), F32)], axis=0)

    tm_in = min(TM_INPROJ, seq)
    tq = min(TQ_ATTN, seq)
    outs = {k: [] for k in ("kp", "vp", "pp", "cp", "ks", "vs", "ps", "cs")}
    for l in range(depth):
        lam_init = 0.8 - 0.6 * math.exp(-0.3 * l)
        w_in_l = w_in[l].astype(BF16)
        w_o_l = w_o[l].astype(BF16)
        w_gu_l = w_gate_up[l].astype(BF16)
        w_dn_l = w_down[l].astype(BF16)
        w_pool_l = w_pool[l].astype(BF16)
        lamp = jnp.stack([lambda_q1[l], lambda_k1[l], lambda_q2[l], lambda_k2[l]])
        g_sub = g_subln[l][None]
        pscale = pool_scale[l][None]
        cw = conv_w[l]
        row = lambda a: a[l][None]

        q_m, ktf_m, ktb_m, vf_m, vb_m, zpu_m, gat_m = _inproj(xm, row(g_pre_mix), w_in_l, tm_in, seq)
        q_s, ktf_s, ktb_s, vf_s, vb_s, zpu_s, gat_s = _inproj(xs, row(g_pre_mix), w_in_l, ns_pad, ns_pad)
        ktf_s, ktb_s = ktf_s[0], ktb_s[0, 0]

        kt_meta = jnp.pad(ktb_s[:, :meta_rows].reshape(ATTN_W, nb, N_META).transpose(1, 0, 2),
                          ((0, 0), (0, 0), (0, PAGE - N_META)))
        v_meta = jnp.pad(vb_s[:meta_rows].reshape(nb, N_META, ATTN_W), ((0, 0), (0, PAGE - N_META), (0, 0)))
        attn = functools.partial(_attention, lamp=lamp, g_subln=g_sub, n_seq=nb, tq=tq, lam_init=lam_init)
        oa_m = attn(q_m, kt_meta, v_meta, ktb_m, vb_m, rows_per_seq=seq, pos_base0=N_META)
        q_meta = jnp.pad(q_s[:meta_rows].reshape(nb, N_META, ATTN_W), ((0, 0), (0, tq - N_META), (0, 0)))
        oa_meta = attn(q_meta.reshape(nb * tq, ATTN_W), kt_meta, v_meta, None, None, rows_per_seq=tq, pos_base0=0)
        oa_meta = oa_meta.reshape(nb, tq, ATTN_W)[:, :N_META].reshape(meta_rows, ATTN_W)

        kt_new = jnp.pad(ktb_s[:, meta_rows:n_small].reshape(ATTN_W, bd, t_dec).transpose(1, 0, 2),
                         ((0, 0), (0, 0), (0, PAGE - t_dec)))
        v_new = jnp.pad(vb_s[meta_rows:n_small].reshape(bd, t_dec, ATTN_W), ((0, 0), (0, PAGE - t_dec), (0, 0)))
        oa_dec = _sample_attention(page_table, q_s[meta_rows:n_small].reshape(bd, t_dec, ATTN_W), kt_new, v_new,
                                   kt_cache, v_cache, lamp, g_sub, layer=l,
                                   pp=math.gcd(PAGES_PER_STEP, page_table.shape[1]), lam_init=lam_init)
        oa_s = jnp.concatenate([oa_meta, oa_dec.reshape(bd * t_dec, ATTN_W),
                                jnp.zeros((ns_pad - n_small, ATTN_W), BF16)], axis=0)

        hist_zpu = zpu_s[:meta_rows].reshape(nb, N_META, ATTN_W)
        hist_gat = gat_s[:meta_rows].reshape(nb, N_META, ATTN_W)
        xm, uc_tail = _mix_main(xm, oa_m, zpu_m, gat_m, hist_zpu, hist_gat, w_o_l, w_pool_l, pscale, cw,
                                row(g_post_mix), tm=min(TM_MIX, seq), rows_per_seq=seq)
        hist_pool = jnp.pad(state_pool[l], ((0, 0), (HIST - state_pool.shape[2], 0), (0, 0)))
        hist_conv = jnp.pad(state_conv[l], ((0, 0), (HIST - state_conv.shape[2], 0), (0, 0)))
        xs, uc_s = _mix_small(xs, oa_s, zpu_s, gat_s, hist_pool, hist_conv, w_o_l, w_pool_l, pscale, cw,
                              row(g_post_mix), n_meta_seq=nb, n_dec=bd, t_dec=t_dec)

        xm = _ffn(xm, row(g_pre_ffn), w_gu_l, w_dn_l, row(g_post_ffn), tm=min(TM_FFN, seq), tc=TC_FFN)
        xs = _ffn(xs, row(g_pre_ffn), w_gu_l, w_dn_l, row(g_post_ffn), tm=ns_pad, tc=TC_FFN)

        kt_p = jnp.concatenate([ktf_s[:, :meta_rows].reshape(ATTN_W, nb, N_META).transpose(1, 0, 2), ktf_m], axis=2)
        outs["kp"].append(kt_p.reshape(nb, N_HEADS, 2, QK_DIM, N_META + seq).transpose(0, 4, 1, 2, 3))
        v_p = jnp.concatenate([vf_s[:meta_rows].reshape(nb, N_META, ATTN_W), vf_m.reshape(nb, seq, ATTN_W)], axis=1)
        outs["vp"].append(v_p.reshape(nb, N_META + seq, N_HEADS, V_DIM))
        z_m = zpu_m.reshape(nb, seq, ATTN_W)
        n_ph = state_pool.shape[2]
        n_ch = state_conv.shape[2]
        outs["pp"].append(z_m[:, seq - n_ph:, :POOL_W])
        outs["cp"].append(uc_tail[:, HIST - n_ch:])
        outs["ks"].append(ktf_s[:, meta_rows:n_small].T.reshape(bd, t_dec, N_HEADS, 2, QK_DIM))
        outs["vs"].append(vf_s[meta_rows:n_small].reshape(bd, t_dec, N_HEADS, V_DIM))
        z_dec = zpu_s[meta_rows:n_small, :POOL_W].reshape(bd, t_dec, POOL_W)
        outs["ps"].append(jnp.concatenate([state_pool[l], z_dec], axis=1)[:, -n_ph:])
        outs["cs"].append(uc_s[meta_rows:n_small].reshape(bd, t_dec, CONV_CH)[:, -n_ch:])

    st = lambda k: jnp.stack(outs[k])
    return (xm.reshape(nb, seq, d), xs[meta_rows:n_small].reshape(bd, t_dec, d),
            st("kp"), st("vp"), st("pp"), st("cp"), st("ks"), st("vs"), st("ps"), st("cs"))
```

```python
import functools
import math

import jax
import jax.numpy as jnp
from jax import lax
from jax.experimental import pallas as pl
from jax.experimental.pallas import tpu as pltpu

N_META = 16
N_HEADS = 8
QK_DIM = 64
V_DIM = 128
HEAD_COLS = 2 * QK_DIM
ATTN_W = N_HEADS * V_DIM
POOL_W = 512
POOL_WINDOWS = (2, 4, 8, 16)
POOL_GROUP = 128
CONV_CH = 512
HIST = 16
EPS = 1e-6
NEG_INF = -1e30
LOG2E = 1.4426950408889634
PAGE = 128

F32 = jnp.float32
BF16 = jnp.bfloat16

VMEM_LIMIT_BYTES = 56 * 1024 * 1024


def _params(sem):
    return pltpu.CompilerParams(dimension_semantics=sem, vmem_limit_bytes=VMEM_LIMIT_BYTES)


def _resident(shape, index_map):
    return pl.BlockSpec(shape, index_map, pipeline_mode=pl.Buffered(1))


def _rms(x, g):
    r = lax.rsqrt(jnp.mean(x * x, axis=-1, keepdims=True) + EPS)
    return x * r * g


def _inproj_kernel(x_ref, g_ref, w_ref, qt_ref, kb_ref, ktf_ref, vf_ref, vtb_ref, zpu_ref, gat_ref):
    h = _rms(x_ref[...], g_ref[...]).astype(BF16)

    def proj(c):
        return jnp.dot(h, w_ref[:, c * ATTN_W:(c + 1) * ATTN_W], preferred_element_type=F32)

    qt_ref[...] = (proj(0) * (LOG2E * QK_DIM ** -0.5)).T.astype(BF16)
    k = proj(1)
    kb_ref[...] = k.astype(BF16)
    ktf_ref[0] = k.T
    v = proj(2)
    vf_ref[...] = v
    vtb_ref[0, 0] = v.T.astype(BF16)
    zpu_ref[...] = proj(3)
    gat_ref[...] = proj(4)


def _inproj(x, g, w, tm, rows_per_seq):
    n, d = x.shape
    assert n % tm == 0 and rows_per_seq % tm == 0 and n % rows_per_seq == 0
    n_seq = n // rows_per_seq
    tps = rows_per_seq // tm
    row = lambda i: (i, 0)
    out_shape = (
        jax.ShapeDtypeStruct((ATTN_W, n), BF16),
        jax.ShapeDtypeStruct((n, ATTN_W), BF16),
        jax.ShapeDtypeStruct((n_seq, ATTN_W, rows_per_seq), F32),
        jax.ShapeDtypeStruct((n, ATTN_W), F32),
        jax.ShapeDtypeStruct((n_seq, tps, ATTN_W, tm), BF16),
        jax.ShapeDtypeStruct((n, ATTN_W), F32),
        jax.ShapeDtypeStruct((n, ATTN_W), F32),
    )
    out_specs = (
        pl.BlockSpec((ATTN_W, tm), lambda i: (0, i)),
        pl.BlockSpec((tm, ATTN_W), row),
        pl.BlockSpec((1, ATTN_W, tm), lambda i: (i // tps, 0, i % tps)),
        pl.BlockSpec((tm, ATTN_W), row),
        pl.BlockSpec((1, 1, ATTN_W, tm), lambda i: (i // tps, i % tps, 0, 0)),
        pl.BlockSpec((tm, ATTN_W), row),
        pl.BlockSpec((tm, ATTN_W), row),
    )
    return pl.pallas_call(
        _inproj_kernel,
        grid=(n // tm,),
        in_specs=[
            pl.BlockSpec((tm, d), row),
            _resident((1, d), lambda i: (0, 0)),
            _resident(w.shape, lambda i: (0, 0)),
        ],
        out_specs=out_specs,
        out_shape=out_shape,
        compiler_params=_params(("parallel",)),
        name="inproj",
    )(x, g, w)


def _lambda(lamp_ref, lam_init):
    lp = lamp_ref[...]
    s1 = jnp.sum(lp[0:1] * lp[1:2], axis=-1, keepdims=True)
    s2 = jnp.sum(lp[2:3] * lp[3:4], axis=-1, keepdims=True)
    return jnp.exp(s1) - jnp.exp(s2) + lam_init


def _attn_kernel(*refs, has_main, pos_base0, lam_init):
    if has_main:
        (qt_ref, km_ref, vtm_ref, k_ref, vt_ref, lamp_ref, g_ref, o_ref,
         q2t_ref, z_ref, nb_ref, m_ref, l_ref, acc_ref) = refs
    else:
        qt_ref, km_ref, vtm_ref, lamp_ref, g_ref, o_ref, q2t_ref, z_ref, m_ref, l_ref, acc_ref = refs
    qi = pl.program_id(1)
    w2 = 2 * TQ

    row = lax.broadcasted_iota(jnp.int32, (HEAD_COLS, TQ), 0)
    for h in range(N_HEADS):
        qth = qt_ref[h * HEAD_COLS:(h + 1) * HEAD_COLS, :]
        zero = jnp.zeros_like(qth)
        q2t_ref[h] = jnp.concatenate([jnp.where(row < QK_DIM, qth, zero), jnp.where(row >= QK_DIM, qth, zero)],
                                     axis=1)
    m_ref[...] = jnp.full(m_ref.shape, NEG_INF, F32)
    l_ref[...] = jnp.zeros(l_ref.shape, F32)
    acc_ref[...] = jnp.zeros(acc_ref.shape, F32)

    def key_block(width, k_blk, vt_blk, bias, c):
        zmax = []
        for h in range(N_HEADS):
            z = jnp.dot(k_blk(h), q2t_ref[h], preferred_element_type=F32) + bias(h)
            z_ref[h, 0:width] = z
            zmax.append(jnp.max(z, axis=0, keepdims=True))
        ones = jnp.ones((16, width), BF16)
        for h in range(N_HEADS):
            m_old = m_ref[h]
            m_new = jnp.maximum(m_old, zmax[h] + c(h))
            alpha = jnp.exp2(m_old - m_new)
            p = jnp.exp2(z_ref[h, 0:width] - (m_new - c(h))).astype(BF16)
            l_ref[h] = alpha * l_ref[h] + jnp.dot(ones, p, preferred_element_type=F32)[0:1]
            acc_ref[h] = alpha * acc_ref[h] + jnp.dot(vt_blk(h), p, preferred_element_type=F32)
            m_ref[h] = m_new

    def heads(h):
        return slice(h * HEAD_COLS, (h + 1) * HEAD_COLS)

    def slope(h):
        return LOG2E * 2.0 ** -(h + 1)

    def q_local(shape):
        c = lax.broadcasted_iota(jnp.int32, shape, 1)
        return jnp.where(c >= TQ, c - TQ, c)

    shape_m = (PAGE, w2)
    kj = lax.broadcasted_iota(jnp.int32, shape_m, 0)
    qpos = q_local(shape_m) + (pos_base0 + qi * TQ)
    vis_m = (kj < N_META) & (kj <= qpos)
    dist_m = (qpos - kj).astype(F32)
    key_block(PAGE, lambda h: km_ref[0, :, heads(h)], lambda h: vtm_ref[0, heads(h), :],
              lambda h: jnp.where(vis_m, dist_m * -slope(h), NEG_INF), lambda h: 0.0)

    if has_main:
        shape = (TK, w2)
        rel = q_local(shape) - lax.broadcasted_iota(jnp.int32, shape, 0)
        relf = rel.astype(F32)
        for h in range(N_HEADS):
            nb_ref[h] = relf * -slope(h)
        n_full = (qi * TQ) // TK

        def main_block(kb, bias):
            off = (qi * TQ - kb * TK).astype(F32)
            start = pl.multiple_of(kb * TK, TK)
            key_block(TK, lambda h: k_ref[pl.ds(start, TK), heads(h)], lambda h: vt_ref[0, kb, heads(h), :],
                      bias, lambda h: off * -slope(h))

        def full_block(kb, carry):
            main_block(kb, lambda h: nb_ref[h])
            return carry

        lax.fori_loop(0, n_full, full_block, 0)

        causal = (rel + (qi * TQ - n_full * TK)) >= 0
        main_block(n_full, lambda h: jnp.where(causal, nb_ref[h], NEG_INF))

    lam = _lambda(lamp_ref, lam_init)
    g_col = jnp.broadcast_to(g_ref[...], (V_DIM, TQ)) * (1.0 - lam_init)
    for h in range(N_HEADS):
        acc = acc_ref[h]
        l = l_ref[h]
        ot = acc[:, :TQ] / l[:, :TQ] - lam * (acc[:, TQ:] / l[:, TQ:])
        r = lax.rsqrt(jnp.mean(ot * ot, axis=0, keepdims=True) + EPS)
        o_ref[:, h * V_DIM:(h + 1) * V_DIM] = ((ot * r) * g_col).T.astype(BF16)


def _attention(qt, k_meta, vt_meta, k_main, vt_main, lamp, g_col, *, n_seq, rows_per_seq, pos_base0, lam_init):
    nq = rows_per_seq // TQ
    has_main = k_main is not None
    in_specs = [
        pl.BlockSpec((ATTN_W, TQ), lambda b, i: (0, b * nq + i)),
        pl.BlockSpec((1, PAGE, ATTN_W), lambda b, i: (b, 0, 0)),
        pl.BlockSpec((1, ATTN_W, PAGE), lambda b, i: (b, 0, 0)),
    ]
    args = [qt, k_meta, vt_meta]
    scratch = [pltpu.VMEM((N_HEADS, HEAD_COLS, 2 * TQ), BF16),
               pltpu.VMEM((N_HEADS, TK if has_main else PAGE, 2 * TQ), F32)]
    if has_main:
        assert rows_per_seq % TK == 0 and TK % TQ == 0
        in_specs += [
            pl.BlockSpec((rows_per_seq, ATTN_W), lambda b, i: (b, 0)),
            pl.BlockSpec((1, rows_per_seq // TK, ATTN_W, TK), lambda b, i: (b, 0, 0, 0)),
        ]
        args += [k_main, vt_main]
        scratch += [pltpu.VMEM((N_HEADS, TK, 2 * TQ), F32)]
    in_specs += [
        pl.BlockSpec((4, QK_DIM), lambda b, i: (0, 0)),
        pl.BlockSpec((V_DIM, 1), lambda b, i: (0, 0)),
    ]
    args += [lamp, g_col]
    scratch += [
        pltpu.VMEM((N_HEADS, 1, 2 * TQ), F32),
        pltpu.VMEM((N_HEADS, 1, 2 * TQ), F32),
        pltpu.VMEM((N_HEADS, V_DIM, 2 * TQ), F32),
    ]
    return pl.pallas_call(
        functools.partial(_attn_kernel, has_main=has_main, pos_base0=pos_base0, lam_init=lam_init),
        grid=(n_seq, nq),
        in_specs=in_specs,
        out_specs=pl.BlockSpec((TQ, ATTN_W), lambda b, i: (b * nq + i, 0)),
        out_shape=jax.ShapeDtypeStruct((n_seq * rows_per_seq, ATTN_W), BF16),
        scratch_shapes=scratch,
        compiler_params=_params(("parallel", "parallel")),
        name="attn_prompt" if has_main else "attn_meta",
    )(*args)


def _sample_attn_kernel(pt_ref, q_ref, ktn_ref, vn_ref, *rest, pp, t_dec, past_len, lam_init):
    kt_refs = rest[:pp]
    v_refs = rest[pp:2 * pp]
    lamp_ref, g_ref, o_ref, qbd_ref, m_ref, l_ref, acc_ref = rest[2 * pp:]
    s_idx = pl.program_id(1)
    rows = N_HEADS * 2 * t_dec
    grp = 2 * t_dec

    r_col = lax.broadcasted_iota(jnp.int32, (rows, 1), 0)
    head = r_col // grp
    t_of_r = (r_col % t_dec).astype(F32)
    slope = lax.bitcast_convert_type((126 - head) << 23, F32) * LOG2E

    def pv_update(z, v_of_head):
        m_old = m_ref[...]
        m_new = jnp.maximum(m_old, jnp.max(z, axis=-1, keepdims=True))
        alpha = jnp.exp2(m_old - m_new)
        p = jnp.exp2(z - m_new)
        l_ref[...] = alpha * l_ref[...] + jnp.sum(p, axis=-1, keepdims=True)
        pb = p.astype(BF16)
        pv = [jnp.dot(pb[hh * grp:(hh + 1) * grp], v_of_head(hh), preferred_element_type=F32)
              for hh in range(N_HEADS)]
        acc_ref[...] = alpha * acc_ref[...] + jnp.concatenate(pv, axis=0)
        m_ref[...] = m_new

    @pl.when(s_idx == 0)
    def _init():
        qf = q_ref[0].astype(F32)
        qt = jnp.concatenate([qf] * (2 * N_HEADS), axis=0)
        rr = lax.broadcasted_iota(jnp.int32, qt.shape, 0) // t_dec
        cc = lax.broadcasted_iota(jnp.int32, qt.shape, 1) // QK_DIM
        qbd_ref[...] = jnp.where(rr == cc, qt, 0.0).astype(BF16)
        m_ref[...] = jnp.full(m_ref.shape, NEG_INF, F32)
        l_ref[...] = jnp.zeros(l_ref.shape, F32)
        acc_ref[...] = jnp.zeros(acc_ref.shape, F32)
        s = jnp.dot(qbd_ref[...], ktn_ref[0], preferred_element_type=F32)
        kj = lax.broadcasted_iota(jnp.int32, s.shape, 1)
        tq_i = lax.broadcasted_iota(jnp.int32, s.shape, 0) % t_dec
        dist = (tq_i - kj).astype(F32)
        z = jnp.where((kj < t_dec) & (kj <= tq_i), s - slope * dist, NEG_INF)
        vn = vn_ref[0]
        pv_update(z, lambda hh: vn[:, hh * V_DIM:(hh + 1) * V_DIM])

    qbd = qbd_ref[...]
    s = jnp.concatenate(
        [jnp.dot(qbd, kt_refs[j][0, 0].astype(BF16), preferred_element_type=F32) for j in range(pp)],
        axis=-1)
    col = lax.broadcasted_iota(jnp.int32, s.shape, 1).astype(F32)
    base = (s_idx * (pp * PAGE)).astype(F32)
    dist = (past_len + t_of_r - base) - col
    z = s - slope * dist

    def v_cached(hh):
        return jnp.concatenate(
            [v_refs[j][0, 0, pl.ds(hh, PAGE, stride=N_HEADS), :].astype(BF16) for j in range(pp)], axis=0)

    pv_update(z, v_cached)

    @pl.when(s_idx == pl.num_programs(1) - 1)
    def _fin():
        lam = _lambda(lamp_ref, lam_init)
        acc = acc_ref[...]
        l = l_ref[...]
        g = g_ref[...]
        outs = []
        for hh in range(N_HEADS):
            a0, a1 = hh * grp, hh * grp + t_dec
            o = acc[a0:a0 + t_dec] / l[a0:a0 + t_dec] - lam * (acc[a1:a1 + t_dec] / l[a1:a1 + t_dec])
            outs.append(_rms(o, g) * (1.0 - lam_init))
        o_ref[0] = jnp.concatenate(outs, axis=-1).astype(BF16)


def _sample_attention(page_table, q_s, kt_new, v_new, kt_cache, v_cache, lamp, g_subln, *, layer, pp,
                      lam_init):
    bd, t_dec, _ = q_s.shape
    n_pages = page_table.shape[1]
    assert n_pages % pp == 0
    rows = N_HEADS * 2 * t_dec

    def page_spec(j):
        return pl.BlockSpec((1, 1, ATTN_W, PAGE), lambda b, s, pt: (layer, pt[b, s * pp + j], 0, 0))

    in_specs = [
        pl.BlockSpec((1, t_dec, ATTN_W), lambda b, s, pt: (b, 0, 0)),
        pl.BlockSpec((1, ATTN_W, PAGE), lambda b, s, pt: (b, 0, 0)),
        pl.BlockSpec((1, PAGE, ATTN_W), lambda b, s, pt: (b, 0, 0)),
    ]
    in_specs += [page_spec(j) for j in range(pp)] + [page_spec(j) for j in range(pp)]
    in_specs += [
        pl.BlockSpec((4, QK_DIM), lambda b, s, pt: (0, 0)),
        pl.BlockSpec((1, V_DIM), lambda b, s, pt: (0, 0)),
    ]
    grid_spec = pltpu.PrefetchScalarGridSpec(
        num_scalar_prefetch=1,
        grid=(bd, n_pages // pp),
        in_specs=in_specs,
        out_specs=pl.BlockSpec((1, t_dec, ATTN_W), lambda b, s, pt: (b, 0, 0)),
        scratch_shapes=[
            pltpu.VMEM((rows, ATTN_W), BF16),
            pltpu.VMEM((rows, 1), F32),
            pltpu.VMEM((rows, 1), F32),
            pltpu.VMEM((rows, V_DIM), F32),
        ],
    )
    return pl.pallas_call(
        functools.partial(_sample_attn_kernel, pp=pp, t_dec=t_dec, past_len=float(n_pages * PAGE),
                          lam_init=lam_init),
        grid_spec=grid_spec,
        out_shape=jax.ShapeDtypeStruct((bd, t_dec, ATTN_W), BF16),
        compiler_params=_params(("parallel", "arbitrary")),
        name="attn_sample",
    )(page_table, q_s, kt_new, v_new, *([kt_cache] * pp), *([v_cache] * pp), lamp, g_subln)


def _seq_mixers(zf_ref, ucf_ref, t, z, u, gate_b, gate_c, wpool_ref, pscale, cw, ramp_cnt):
    zf_ref[HIST:HIST + t] = z
    uc = gate_c * u
    ucf_ref[HIST:HIST + t] = uc
    pooled_out = []
    for gi, w in enumerate(POOL_WINDOWS):
        cols = pl.ds(gi * POOL_GROUP, POOL_GROUP)
        zg = z[:, gi * POOL_GROUP:(gi + 1) * POOL_GROUP]
        acc = zg
        for j in range(1, w):
            acc = acc + zf_ref[pl.ds(HIST - j, t), cols]
        if ramp_cnt:
            pos = lax.broadcasted_iota(jnp.int32, (t, 1), 0)
            cnt = jnp.minimum(pos + 1, w).astype(F32)
            pooled = acc / cnt - zg
        else:
            pooled = acc * (1.0 / w) - zg
        pooled_out.append(jnp.dot(pooled.astype(BF16), wpool_ref[gi], preferred_element_type=F32))
    o_pool = jnp.concatenate(pooled_out, axis=-1) * pscale
    y = cw[0:1] * ucf_ref[pl.ds(HIST - 2, t), :] + cw[1:2] * ucf_ref[pl.ds(HIST - 1, t), :] + cw[2:3] * uc
    return o_pool, gate_b * y, uc


def _mix_project(x, o_attn, o_pool, o_conv, wo_ref, g_post):
    mix = jnp.dot(o_attn, wo_ref[0:ATTN_W], preferred_element_type=F32)
    mix += jnp.dot(o_pool.astype(BF16), wo_ref[ATTN_W:ATTN_W + POOL_W], preferred_element_type=F32)
    mix += jnp.dot(o_conv.astype(BF16), wo_ref[ATTN_W + POOL_W:], preferred_element_type=F32)
    return x + _rms(mix, g_post)


def _mix_main_kernel(x_ref, oa_ref, zpu_ref, gat_ref, hz_ref, hg_ref, wo_ref, wpool_ref, ps_ref, cw_ref,
                     g_ref, xo_ref, uct_ref, zf_ref, ucf_ref, *, tm, tiles_per_seq):
    i = pl.program_id(0)

    @pl.when(i % tiles_per_seq == 0)
    def _():
        hz = hz_ref[0]
        hg = hg_ref[0]
        zf_ref[0:HIST] = hz[:, :POOL_W]
        ucf_ref[0:HIST] = hg[:, CONV_CH:] * hz[:, POOL_W:]

    @pl.when(i % tiles_per_seq != 0)
    def _():
        zf_ref[0:HIST] = zf_ref[tm:tm + HIST]
        ucf_ref[0:HIST] = ucf_ref[tm:tm + HIST]

    zpu = zpu_ref[...]
    gat = gat_ref[...]
    o_pool, o_conv, uc = _seq_mixers(zf_ref, ucf_ref, tm, zpu[:, :POOL_W], zpu[:, POOL_W:], gat[:, :CONV_CH],
                                     gat[:, CONV_CH:], wpool_ref, ps_ref[...], cw_ref[...], False)
    uct_ref[0] = uc[tm - HIST:]
    xo_ref[...] = _mix_project(x_ref[...], oa_ref[...], o_pool, o_conv, wo_ref, g_ref[...])


def _mix_main(x, o_attn, zpu, gates, hist_zpu, hist_gates, wo, wpool, pscale, cw, g_post, *, tm, rows_per_seq):
    n, d = x.shape
    tps = rows_per_seq // tm
    n_seq = n // rows_per_seq
    row = lambda i: (i, 0)
    const2 = lambda i: (0, 0)
    return pl.pallas_call(
        functools.partial(_mix_main_kernel, tm=tm, tiles_per_seq=tps),
        grid=(n // tm,),
        in_specs=[
            pl.BlockSpec((tm, d), row),
            pl.BlockSpec((tm, ATTN_W), row),
            pl.BlockSpec((tm, ATTN_W), row),
            pl.BlockSpec((tm, ATTN_W), row),
            pl.BlockSpec((1, HIST, ATTN_W), lambda i: (i // tps, 0, 0)),
            pl.BlockSpec((1, HIST, ATTN_W), lambda i: (i // tps, 0, 0)),
            _resident(wo.shape, const2),
            _resident(wpool.shape, lambda i: (0, 0, 0)),
            _resident(pscale.shape, const2),
            _resident(cw.shape, const2),
            _resident(g_post.shape, const2),
        ],
        out_specs=(
            pl.BlockSpec((tm, d), row),
            pl.BlockSpec((1, HIST, CONV_CH), lambda i: (i // tps, 0, 0)),
        ),
        out_shape=(
            jax.ShapeDtypeStruct((n, d), F32),
            jax.ShapeDtypeStruct((n_seq, HIST, CONV_CH), F32),
        ),
        scratch_shapes=[pltpu.VMEM((HIST + tm, POOL_W), F32), pltpu.VMEM((HIST + tm, CONV_CH), F32)],
        compiler_params=_params(("arbitrary",)),
        name="mix_main",
    )(x, o_attn, zpu, gates, hist_zpu, hist_gates, wo, wpool, pscale, cw, g_post)


def _mix_small_kernel(x_ref, oa_ref, zpu_ref, gat_ref, hp_ref, hc_ref, wo_ref, wpool_ref, ps_ref, cw_ref,
                      g_ref, xo_ref, uc_ref, zf_ref, ucf_ref, *, n_meta_seq, n_dec, t_dec):
    pools, convs, ucs = [], [], []
    segs = [(s * N_META, N_META, None) for s in range(n_meta_seq)]
    segs += [(n_meta_seq * N_META + b * t_dec, t_dec, b) for b in range(n_dec)]
    for start, t, b in segs:
        if b is None:
            zf_ref[0:HIST] = jnp.zeros((HIST, POOL_W), F32)
            ucf_ref[0:HIST] = jnp.zeros((HIST, CONV_CH), F32)
        else:
            zf_ref[0:HIST] = hp_ref[b]
            ucf_ref[0:HIST] = hc_ref[b]
        zpu = zpu_ref[start:start + t]
        gat = gat_ref[start:start + t]
        o_pool, o_conv, uc = _seq_mixers(zf_ref, ucf_ref, t, zpu[:, :POOL_W], zpu[:, POOL_W:], gat[:, :CONV_CH],
                                         gat[:, CONV_CH:], wpool_ref, ps_ref[...], cw_ref[...], b is None)
        pools.append(o_pool)
        convs.append(o_conv)
        ucs.append(uc)
    n_pad = x_ref.shape[0] - (n_meta_seq * N_META + n_dec * t_dec)
    if n_pad:
        pad = jnp.zeros((n_pad, POOL_W), F32)
        pools.append(pad)
        convs.append(pad)
        ucs.append(pad)
    uc_ref[...] = jnp.concatenate(ucs, axis=0)
    xo_ref[...] = _mix_project(x_ref[...], oa_ref[...], jnp.concatenate(pools, axis=0),
                               jnp.concatenate(convs, axis=0), wo_ref, g_ref[...])


def _mix_small(x, o_attn, zpu, gates, hist_pool, hist_conv, wo, wpool, pscale, cw, g_post, *, n_meta_seq,
               n_dec, t_dec):
    n, d = x.shape
    return pl.pallas_call(
        functools.partial(_mix_small_kernel, n_meta_seq=n_meta_seq, n_dec=n_dec, t_dec=t_dec),
        out_shape=(jax.ShapeDtypeStruct((n, d), F32), jax.ShapeDtypeStruct((n, CONV_CH), F32)),
        scratch_shapes=[pltpu.VMEM((HIST + N_META, POOL_W), F32), pltpu.VMEM((HIST + N_META, CONV_CH), F32)],
        compiler_params=pltpu.CompilerParams(vmem_limit_bytes=VMEM_LIMIT_BYTES),
        name="mix_small",
    )(x, o_attn, zpu, gates, hist_pool, hist_conv, wo, wpool, pscale, cw, g_post)


def _ffn_kernel(x_ref, gpre_ref, wg_ref, wu_ref, wd_ref, gpost_ref, o_ref, h_ref, acc_ref):
    c = pl.program_id(1)

    @pl.when(c == 0)
    def _():
        h_ref[...] = _rms(x_ref[...], gpre_ref[...]).astype(BF16)
        acc_ref[...] = jnp.zeros(acc_ref.shape, F32)

    h = h_ref[...]
    gate = jnp.dot(h, wg_ref[...], preferred_element_type=F32)
    up = jnp.dot(h, wu_ref[...], preferred_element_type=F32)
    act = (gate * jax.nn.sigmoid(gate) * up).astype(BF16)
    acc_ref[...] += jnp.dot(act, wd_ref[...], preferred_element_type=F32)

    @pl.when(c == pl.num_programs(1) - 1)
    def _():
        o_ref[...] = x_ref[...] + _rms(acc_ref[...], gpost_ref[...])


def _ffn(x, g_pre, w_gate_up, w_down, g_post, *, tm, tc):
    n, d = x.shape
    hidden = w_down.shape[0]
    assert n % tm == 0 and hidden % tc == 0
    nc = hidden // tc
    return pl.pallas_call(
        _ffn_kernel,
        grid=(n // tm, nc),
        in_specs=[
            pl.BlockSpec((tm, d), lambda i, c: (i, 0)),
            pl.BlockSpec((1, d), lambda i, c: (0, 0)),
            pl.BlockSpec((d, tc), lambda i, c: (0, c)),
            pl.BlockSpec((d, tc), lambda i, c: (0, nc + c)),
            pl.BlockSpec((tc, d), lambda i, c: (c, 0)),
            pl.BlockSpec((1, d), lambda i, c: (0, 0)),
        ],
        out_specs=pl.BlockSpec((tm, d), lambda i, c: (i, 0)),
        out_shape=jax.ShapeDtypeStruct((n, d), F32),
        scratch_shapes=[pltpu.VMEM((tm, d), BF16), pltpu.VMEM((tm, d), F32)],
        compiler_params=_params(("parallel", "arbitrary")),
        name="ffn",
    )(x, g_pre, w_gate_up, w_gate_up, w_down, g_post)


TQ = 128
TK = 256
TM_MIX = 512
TM_FFN = 512
TC_FFN = 512
PAGES_PER_STEP = 4


def kernel(x_prompt, x_sample, cache_k, cache_v, state_pool, state_conv, page_table, meta_tokens, w_in, w_o,
           lambda_q1, lambda_k1, lambda_q2, lambda_k2, g_subln, w_pool, pool_scale, conv_w, w_gate_up, w_down,
           g_pre_mix, g_post_mix, g_pre_ffn, g_post_ffn):
    nb, seq, d = x_prompt.shape
    bd, t_dec, _ = x_sample.shape
    depth = w_in.shape[0]
    n_pool = cache_k.shape[1]
    n_small = nb * N_META + bd * t_dec
    meta_rows = nb * N_META
    ns_pad = -(-n_small // PAGE) * PAGE

    kt_cache = jnp.transpose(cache_k, (0, 1, 3, 4, 5, 2)).reshape(depth, n_pool, ATTN_W, PAGE)
    v_cache = cache_v.reshape(depth, n_pool, PAGE * N_HEADS, V_DIM)

    xm = x_prompt.reshape(nb * seq, d)
    xs = jnp.concatenate([jnp.broadcast_to(meta_tokens[None], (nb, N_META, d)).reshape(meta_rows, d),
                          x_sample.reshape(bd * t_dec, d), jnp.zeros((ns_pad - n_small, d), F32)], axis=0)

    outs = {k: [] for k in ("kp", "vp", "pp", "cp", "ks", "vs", "ps", "cs")}
    for l in range(depth):
        lam_init = 0.8 - 0.6 * math.exp(-0.3 * l)
        w_in_l = w_in[l].astype(BF16)
        w_o_l = w_o[l].astype(BF16)
        w_gu_l = w_gate_up[l].astype(BF16)
        w_dn_l = w_down[l].astype(BF16)
        w_pool_l = w_pool[l].astype(BF16)
        lamp = jnp.stack([lambda_q1[l], lambda_k1[l], lambda_q2[l], lambda_k2[l]])
        g_sub = g_subln[l][None]
        g_col = g_subln[l][:, None]
        pscale = pool_scale[l][None]
        cw = conv_w[l]
        row = lambda a: a[l][None]

        qt_m, kb_m, ktf_m, vf_m, vtb_m, zpu_m, gat_m = _inproj(xm, row(g_pre_mix), w_in_l, TK, seq)
        qt_s, kb_s, ktf_s, vf_s, vtb_s, zpu_s, gat_s = _inproj(xs, row(g_pre_mix), w_in_l, ns_pad, ns_pad)
        ktf_s, vtb_s = ktf_s[0], vtb_s[0, 0]

        k_meta = jnp.pad(kb_s[:meta_rows].reshape(nb, N_META, ATTN_W), ((0, 0), (0, PAGE - N_META), (0, 0)))
        vt_meta = jnp.pad(vtb_s[:, :meta_rows].reshape(ATTN_W, nb, N_META).transpose(1, 0, 2),
                          ((0, 0), (0, 0), (0, PAGE - N_META)))
        attn = functools.partial(_attention, lamp=lamp, g_col=g_col, n_seq=nb, lam_init=lam_init)
        oa_m = attn(qt_m, k_meta, vt_meta, kb_m, vtb_m, rows_per_seq=seq, pos_base0=N_META)
        qt_meta = jnp.pad(qt_s[:, :meta_rows].reshape(ATTN_W, nb, N_META), ((0, 0), (0, 0), (0, TQ - N_META)))
        oa_meta = attn(qt_meta.reshape(ATTN_W, nb * TQ), k_meta, vt_meta, None, None, rows_per_seq=TQ, pos_base0=0)
        oa_meta = oa_meta.reshape(nb, TQ, ATTN_W)[:, :N_META].reshape(meta_rows, ATTN_W)

        q_dec = qt_s[:, meta_rows:n_small].T.reshape(bd, t_dec, ATTN_W)
        kt_new = jnp.pad(ktf_s[:, meta_rows:n_small].astype(BF16).reshape(ATTN_W, bd, t_dec).transpose(1, 0, 2),
                         ((0, 0), (0, 0), (0, PAGE - t_dec)))
        v_new = jnp.pad(vf_s[meta_rows:n_small].astype(BF16).reshape(bd, t_dec, ATTN_W),
                        ((0, 0), (0, PAGE - t_dec), (0, 0)))
        oa_dec = _sample_attention(page_table, q_dec, kt_new, v_new, kt_cache, v_cache, lamp, g_sub, layer=l,
                                   pp=math.gcd(PAGES_PER_STEP, page_table.shape[1]), lam_init=lam_init)
        oa_s = jnp.concatenate([oa_meta, oa_dec.reshape(bd * t_dec, ATTN_W),
                                jnp.zeros((ns_pad - n_small, ATTN_W), BF16)], axis=0)

        hist_zpu = zpu_s[:meta_rows].reshape(nb, N_META, ATTN_W)
        hist_gat = gat_s[:meta_rows].reshape(nb, N_META, ATTN_W)
        xm, uc_tail = _mix_main(xm, oa_m, zpu_m, gat_m, hist_zpu, hist_gat, w_o_l, w_pool_l, pscale, cw,
                                row(g_post_mix), tm=min(TM_MIX, seq), rows_per_seq=seq)
        hist_pool = jnp.pad(state_pool[l], ((0, 0), (HIST - state_pool.shape[2], 0), (0, 0)))
        hist_conv = jnp.pad(state_conv[l], ((0, 0), (HIST - state_conv.shape[2], 0), (0, 0)))
        xs, uc_s = _mix_small(xs, oa_s, zpu_s, gat_s, hist_pool, hist_conv, w_o_l, w_pool_l, pscale, cw,
                              row(g_post_mix), n_meta_seq=nb, n_dec=bd, t_dec=t_dec)

        xm = _ffn(xm, row(g_pre_ffn), w_gu_l, w_dn_l, row(g_post_ffn), tm=min(TM_FFN, seq), tc=TC_FFN)
        xs = _ffn(xs, row(g_pre_ffn), w_gu_l, w_dn_l, row(g_post_ffn), tm=ns_pad, tc=TC_FFN)

        kt_p = jnp.concatenate([ktf_s[:, :meta_rows].reshape(ATTN_W, nb, N_META).transpose(1, 0, 2), ktf_m], axis=2)
        outs["kp"].append(kt_p.reshape(nb, N_HEADS, 2, QK_DIM, N_META + seq).transpose(0, 4, 1, 2, 3))
        v_p = jnp.concatenate([vf_s[:meta_rows].reshape(nb, N_META, ATTN_W), vf_m.reshape(nb, seq, ATTN_W)], axis=1)
        outs["vp"].append(v_p.reshape(nb, N_META + seq, N_HEADS, V_DIM))
        z_m = zpu_m.reshape(nb, seq, ATTN_W)
        n_ph = state_pool.shape[2]
        n_ch = state_conv.shape[2]
        outs["pp"].append(z_m[:, seq - n_ph:, :POOL_W])
        outs["cp"].append(uc_tail[:, HIST - n_ch:])
        outs["ks"].append(ktf_s[:, meta_rows:n_small].T.reshape(bd, t_dec, N_HEADS, 2, QK_DIM))
        outs["vs"].append(vf_s[meta_rows:n_small].reshape(bd, t_dec, N_HEADS, V_DIM))
        z_dec = zpu_s[meta_rows:n_small, :POOL_W].reshape(bd, t_dec, POOL_W)
        outs["ps"].append(jnp.concatenate([state_pool[l], z_dec], axis=1)[:, -n_ph:])
        outs["cs"].append(uc_s[meta_rows:n_small].reshape(bd, t_dec, CONV_CH)[:, -n_ch:])

    st = lambda k: jnp.stack(outs[k])
    return (xm.reshape(nb, seq, d), xs[meta_rows:n_small].reshape(bd, t_dec, d),
            st("kp"), st("vp"), st("pp"), st("cp"), st("ks"), st("vs"), st("ps"), st("cs"))
```

```python
import functools
import math

import jax
import jax.numpy as jnp
import numpy as np
from jax import lax
from jax.experimental import pallas as pl
from jax.experimental.pallas import tpu as pltpu

N_META = 16
N_HEADS = 8
QK_DIM = 64
V_DIM = 128
HEAD_COLS = 2 * QK_DIM
ATTN_W = N_HEADS * V_DIM
POOL_W = 512
POOL_WINDOWS = (2, 4, 8, 16)
POOL_GROUP = 128
CONV_CH = 512
HIST = 16
EPS = 1e-6
NEG_INF = -1e30
LOG2E = 1.4426950408889634
PAGE = 128

F32 = jnp.float32
BF16 = jnp.bfloat16

VMEM_LIMIT_BYTES = 56 * 1024 * 1024


def _params(sem):
    return pltpu.CompilerParams(dimension_semantics=sem, vmem_limit_bytes=VMEM_LIMIT_BYTES)


def _resident(shape, index_map):
    return pl.BlockSpec(shape, index_map, pipeline_mode=pl.Buffered(1))


def _rms(x, g):
    r = lax.rsqrt(jnp.mean(x * x, axis=-1, keepdims=True) + EPS)
    return x * r * g


def _inproj_kernel(x_ref, g_ref, w_ref, qt_ref, kb_ref, ktf_ref, vf_ref, vtb_ref, zpu_ref, gat_ref):
    h = _rms(x_ref[...], g_ref[...]).astype(BF16)

    def proj(c):
        return jnp.dot(h, w_ref[:, c * ATTN_W:(c + 1) * ATTN_W], preferred_element_type=F32)

    qt_ref[...] = (proj(0) * (LOG2E * QK_DIM ** -0.5)).T.astype(BF16)
    k = proj(1)
    kb_ref[...] = k.astype(BF16)
    ktf_ref[0] = k.T
    v = proj(2)
    vf_ref[...] = v
    vtb_ref[0, 0] = v.T.astype(BF16)
    zpu_ref[...] = proj(3)
    gat_ref[...] = proj(4)


def _layer_weight(w, layer):
    return _resident((None,) + w.shape[1:], lambda *_: (layer,) + (0,) * (w.ndim - 1))


def _inproj(x, g, w, tm, rows_per_seq, layer):
    n, d = x.shape
    assert n % tm == 0 and rows_per_seq % tm == 0 and n % rows_per_seq == 0
    n_seq = n // rows_per_seq
    tps = rows_per_seq // tm
    row = lambda i: (i, 0)
    out_shape = (
        jax.ShapeDtypeStruct((ATTN_W, n), BF16),
        jax.ShapeDtypeStruct((n, ATTN_W), BF16),
        jax.ShapeDtypeStruct((n_seq, ATTN_W, rows_per_seq), F32),
        jax.ShapeDtypeStruct((n, ATTN_W), F32),
        jax.ShapeDtypeStruct((n_seq, tps, ATTN_W, tm), BF16),
        jax.ShapeDtypeStruct((n, ATTN_W), F32),
        jax.ShapeDtypeStruct((n, ATTN_W), F32),
    )
    out_specs = (
        pl.BlockSpec((ATTN_W, tm), lambda i: (0, i)),
        pl.BlockSpec((tm, ATTN_W), row),
        pl.BlockSpec((1, ATTN_W, tm), lambda i: (i // tps, 0, i % tps)),
        pl.BlockSpec((tm, ATTN_W), row),
        pl.BlockSpec((1, 1, ATTN_W, tm), lambda i: (i // tps, i % tps, 0, 0)),
        pl.BlockSpec((tm, ATTN_W), row),
        pl.BlockSpec((tm, ATTN_W), row),
    )
    return pl.pallas_call(
        _inproj_kernel,
        grid=(n // tm,),
        in_specs=[
            pl.BlockSpec((tm, d), row),
            _resident((1, d), lambda i: (0, 0)),
            _layer_weight(w, layer),
        ],
        out_specs=out_specs,
        out_shape=out_shape,
        compiler_params=_params(("parallel",)),
        name="inproj",
    )(x, g, w)


BIAS_ROWS = 16


def _bf16_parts(x):
    parts, rest = [], float(x)
    for _ in range(3):
        bits = np.array(rest, np.float32).view(np.uint32)
        bits = (bits + np.uint32(0x7FFF) + ((bits >> np.uint32(16)) & np.uint32(1))) & np.uint32(0xFFFF0000)
        part = float(bits.view(np.float32))
        parts.append(part)
        rest -= part
    return parts


def _lambda(lamp_ref, lam_init):
    lp = lamp_ref[...]
    s1 = jnp.sum(lp[0:1] * lp[1:2], axis=-1, keepdims=True)
    s2 = jnp.sum(lp[2:3] * lp[3:4], axis=-1, keepdims=True)
    return jnp.exp(s1) - jnp.exp(s2) + lam_init


def _attn_kernel(*refs, has_main, pos_base0, lam_init):
    if has_main:
        (qt_ref, km_ref, vtm_ref, k_ref, vt_ref, lamp_ref, g_ref, o_ref,
         q2t_ref, z_ref, ka_ref, m_ref, l_ref, acc_ref) = refs
    else:
        qt_ref, km_ref, vtm_ref, lamp_ref, g_ref, o_ref, q2t_ref, z_ref, m_ref, l_ref, acc_ref = refs
    qi = pl.program_id(1)
    w2 = 2 * TQ

    def heads(h):
        return slice(h * HEAD_COLS, (h + 1) * HEAD_COLS)

    def slope(h):
        return LOG2E * 2.0 ** -(h + 1)

    def q_local(shape):
        c = lax.broadcasted_iota(jnp.int32, shape, 1)
        return jnp.where(c >= TQ, c - TQ, c)

    row = lax.broadcasted_iota(jnp.int32, (HEAD_COLS, TQ), 0)
    if has_main:
        xrow = lax.broadcasted_iota(jnp.int32, (BIAS_ROWS, w2), 0)
        neg_q = -q_local((BIAS_ROWS, w2)).astype(F32)
        kcol = lax.broadcasted_iota(jnp.int32, (TK, HEAD_COLS), 1)
        kloc = lax.broadcasted_iota(jnp.int32, (TK, HEAD_COLS), 0).astype(F32)
    for h in range(N_HEADS):
        qth = qt_ref[heads(h), :]
        zero = jnp.zeros_like(qth)
        q2t_ref[h, 0:HEAD_COLS] = jnp.concatenate(
            [jnp.where(row < QK_DIM, qth, zero), jnp.where(row >= QK_DIM, qth, zero)], axis=1)
        if has_main:
            s1, s2, s3 = _bf16_parts(slope(h))
            ext = jnp.where(xrow == 0, s1, jnp.where(xrow == 1, s2, jnp.where(xrow == 2, s3,
                            jnp.where(xrow < 6, neg_q, 0.0))))
            q2t_ref[h, HEAD_COLS:HEAD_COLS + BIAS_ROWS] = ext.astype(BF16)
            q2t_ref[h, HEAD_COLS + BIAS_ROWS:] = jnp.zeros((HEAD_COLS - BIAS_ROWS, w2), BF16)
            ka = jnp.where(kcol < 3, kloc, jnp.where(kcol == 3, s1, jnp.where(kcol == 4, s2,
                           jnp.where(kcol == 5, s3, 0.0))))
            ka_ref[h] = ka.astype(BF16)
    m_ref[...] = jnp.full(m_ref.shape, NEG_INF, F32)
    l_ref[...] = jnp.zeros(l_ref.shape, F32)
    acc_ref[...] = jnp.zeros(acc_ref.shape, F32)

    def key_block(width, score, vt_blk, c):
        zmax = []
        for h in range(N_HEADS):
            z = score(h)
            z_ref[h, 0:width] = z
            zmax.append(jnp.max(z, axis=0, keepdims=True))
        ones = jnp.ones((16, width), BF16)
        for h in range(N_HEADS):
            m_old = m_ref[h]
            m_new = jnp.maximum(m_old, zmax[h] + c(h))
            alpha = jnp.exp2(m_old - m_new)
            p = jnp.exp2(z_ref[h, 0:width] - (m_new - c(h))).astype(BF16)
            l_ref[h] = alpha * l_ref[h] + jnp.dot(ones, p, preferred_element_type=F32)[0:1]
            acc_ref[h] = alpha * acc_ref[h] + jnp.dot(vt_blk(h), p, preferred_element_type=F32)
            m_ref[h] = m_new

    shape_m = (N_META, w2)
    kj = lax.broadcasted_iota(jnp.int32, shape_m, 0)
    qpos = q_local(shape_m) + (pos_base0 + qi * TQ)
    vis_m = kj <= qpos
    dist_m = (qpos - kj).astype(F32)

    def meta_score(h):
        s = jnp.dot(km_ref[0, :, heads(h)], q2t_ref[h, 0:HEAD_COLS], preferred_element_type=F32)
        return s + jnp.where(vis_m, dist_m * -slope(h), NEG_INF)

    key_block(N_META, meta_score, lambda h: vtm_ref[0, heads(h), :], lambda h: 0.0)

    if has_main:
        n_full = (qi * TQ) // TK

        def main_block(kb, mask):
            off = (qi * TQ - kb * TK).astype(F32)
            start = pl.multiple_of(kb * TK, TK)

            def score(h):
                parts = []
                for r0 in range(0, TK, V_DIM):
                    k_aug = jnp.concatenate([k_ref[pl.ds(start + r0, V_DIM), heads(h)],
                                             ka_ref[h, r0:r0 + V_DIM]], axis=1)
                    parts.append(jnp.dot(k_aug, q2t_ref[h], preferred_element_type=F32))
                return mask(jnp.concatenate(parts, axis=0))

            key_block(TK, score, lambda h: vt_ref[0, kb, heads(h), :], lambda h: off * -slope(h))

        def full_block(kb, carry):
            main_block(kb, lambda z: z)
            return carry

        lax.fori_loop(0, n_full, full_block, 0)

        shape = (TK, w2)
        rel = q_local(shape) - lax.broadcasted_iota(jnp.int32, shape, 0)
        causal = (rel + (qi * TQ - n_full * TK)) >= 0
        main_block(n_full, lambda z: jnp.where(causal, z, NEG_INF))

    lam = _lambda(lamp_ref, lam_init)
    g_col = jnp.broadcast_to(g_ref[...], (V_DIM, TQ)) * (1.0 - lam_init)
    for h in range(N_HEADS):
        acc = acc_ref[h]
        l = l_ref[h]
        ot = acc[:, :TQ] / l[:, :TQ] - lam * (acc[:, TQ:] / l[:, TQ:])
        r = lax.rsqrt(jnp.mean(ot * ot, axis=0, keepdims=True) + EPS)
        o_ref[:, h * V_DIM:(h + 1) * V_DIM] = ((ot * r) * g_col).T.astype(BF16)


def _attention(qt, k_meta, vt_meta, k_main, vt_main, lamp, g_col, *, n_seq, rows_per_seq, pos_base0, lam_init):
    nq = rows_per_seq // TQ
    has_main = k_main is not None
    in_specs = [
        pl.BlockSpec((ATTN_W, TQ), lambda b, i: (0, b * nq + i)),
        pl.BlockSpec((1, N_META, ATTN_W), lambda b, i: (b, 0, 0)),
        pl.BlockSpec((1, ATTN_W, N_META), lambda b, i: (b, 0, 0)),
    ]
    args = [qt, k_meta, vt_meta]
    scratch = [pltpu.VMEM((N_HEADS, 2 * HEAD_COLS if has_main else HEAD_COLS, 2 * TQ), BF16),
               pltpu.VMEM((N_HEADS, TK if has_main else N_META, 2 * TQ), F32)]
    if has_main:
        assert rows_per_seq % TK == 0 and TK % TQ == 0
        in_specs += [
            pl.BlockSpec((rows_per_seq, ATTN_W), lambda b, i: (b, 0)),
            pl.BlockSpec((1, rows_per_seq // TK, ATTN_W, TK), lambda b, i: (b, 0, 0, 0)),
        ]
        args += [k_main, vt_main]
        scratch += [pltpu.VMEM((N_HEADS, TK, HEAD_COLS), BF16)]
    in_specs += [
        pl.BlockSpec((4, QK_DIM), lambda b, i: (0, 0)),
        pl.BlockSpec((V_DIM, 1), lambda b, i: (0, 0)),
    ]
    args += [lamp, g_col]
    scratch += [
        pltpu.VMEM((N_HEADS, 1, 2 * TQ), F32),
        pltpu.VMEM((N_HEADS, 1, 2 * TQ), F32),
        pltpu.VMEM((N_HEADS, V_DIM, 2 * TQ), F32),
    ]
    return pl.pallas_call(
        functools.partial(_attn_kernel, has_main=has_main, pos_base0=pos_base0, lam_init=lam_init),
        grid=(n_seq, nq),
        in_specs=in_specs,
        out_specs=pl.BlockSpec((TQ, ATTN_W), lambda b, i: (b * nq + i, 0)),
        out_shape=jax.ShapeDtypeStruct((n_seq * rows_per_seq, ATTN_W), BF16),
        scratch_shapes=scratch,
        compiler_params=_params(("parallel", "parallel")),
        name="attn_prompt" if has_main else "attn_meta",
    )(*args)


def _sample_attn_kernel(pt_ref, q_ref, ktn_ref, vn_ref, *rest, pp, t_dec, past_len, lam_init):
    kt_refs = rest[:pp]
    v_refs = rest[pp:2 * pp]
    lamp_ref, g_ref, o_ref, qbd_ref, m_ref, l_ref, acc_ref = rest[2 * pp:]
    s_idx = pl.program_id(1)
    rows = N_HEADS * 2 * t_dec
    grp = 2 * t_dec

    r_col = lax.broadcasted_iota(jnp.int32, (rows, 1), 0)
    head = r_col // grp
    t_of_r = (r_col % t_dec).astype(F32)
    slope = lax.bitcast_convert_type((126 - head) << 23, F32) * LOG2E

    def pv_update(z, v_of_head):
        m_old = m_ref[...]
        m_new = jnp.maximum(m_old, jnp.max(z, axis=-1, keepdims=True))
        alpha = jnp.exp2(m_old - m_new)
        p = jnp.exp2(z - m_new)
        l_ref[...] = alpha * l_ref[...] + jnp.sum(p, axis=-1, keepdims=True)
        pb = p.astype(BF16)
        pv = [jnp.dot(pb[hh * grp:(hh + 1) * grp], v_of_head(hh), preferred_element_type=F32)
              for hh in range(N_HEADS)]
        acc_ref[...] = alpha * acc_ref[...] + jnp.concatenate(pv, axis=0)
        m_ref[...] = m_new

    @pl.when(s_idx == 0)
    def _init():
        qf = q_ref[0].astype(F32)
        qt = jnp.concatenate([qf] * (2 * N_HEADS), axis=0)
        rr = lax.broadcasted_iota(jnp.int32, qt.shape, 0) // t_dec
        cc = lax.broadcasted_iota(jnp.int32, qt.shape, 1) // QK_DIM
        qbd_ref[...] = jnp.where(rr == cc, qt, 0.0).astype(BF16)
        m_ref[...] = jnp.full(m_ref.shape, NEG_INF, F32)
        l_ref[...] = jnp.zeros(l_ref.shape, F32)
        acc_ref[...] = jnp.zeros(acc_ref.shape, F32)
        s = jnp.dot(qbd_ref[...], ktn_ref[0], preferred_element_type=F32)
        kj = lax.broadcasted_iota(jnp.int32, s.shape, 1)
        tq_i = lax.broadcasted_iota(jnp.int32, s.shape, 0) % t_dec
        dist = (tq_i - kj).astype(F32)
        z = jnp.where((kj < t_dec) & (kj <= tq_i), s - slope * dist, NEG_INF)
        vn = vn_ref[0]
        pv_update(z, lambda hh: vn[:, hh * V_DIM:(hh + 1) * V_DIM])

    qbd = qbd_ref[...]
    s = jnp.concatenate(
        [jnp.dot(qbd, kt_refs[j][0, 0].astype(BF16), preferred_element_type=F32) for j in range(pp)],
        axis=-1)
    col = lax.broadcasted_iota(jnp.int32, s.shape, 1).astype(F32)
    base = (s_idx * (pp * PAGE)).astype(F32)
    dist = (past_len + t_of_r - base) - col
    z = s - slope * dist

    def v_cached(hh):
        return jnp.concatenate(
            [v_refs[j][0, 0, pl.ds(hh, PAGE, stride=N_HEADS), :].astype(BF16) for j in range(pp)], axis=0)

    pv_update(z, v_cached)

    @pl.when(s_idx == pl.num_programs(1) - 1)
    def _fin():
        lam = _lambda(lamp_ref, lam_init)
        acc = acc_ref[...]
        l = l_ref[...]
        g = g_ref[...]
        outs = []
        for hh in range(N_HEADS):
            a0, a1 = hh * grp, hh * grp + t_dec
            o = acc[a0:a0 + t_dec] / l[a0:a0 + t_dec] - lam * (acc[a1:a1 + t_dec] / l[a1:a1 + t_dec])
            outs.append(_rms(o, g) * (1.0 - lam_init))
        o_ref[0] = jnp.concatenate(outs, axis=-1).astype(BF16)


def _sample_attention(page_table, q_s, kt_new, v_new, kt_cache, v_cache, lamp, g_subln, *, layer, pp,
                      lam_init):
    bd, t_dec, _ = q_s.shape
    n_pages = page_table.shape[1]
    assert n_pages % pp == 0
    rows = N_HEADS * 2 * t_dec

    def page_spec(j):
        return pl.BlockSpec((1, 1, ATTN_W, PAGE), lambda b, s, pt: (layer, pt[b, s * pp + j], 0, 0))

    in_specs = [
        pl.BlockSpec((1, t_dec, ATTN_W), lambda b, s, pt: (b, 0, 0)),
        pl.BlockSpec((1, ATTN_W, PAGE), lambda b, s, pt: (b, 0, 0)),
        pl.BlockSpec((1, PAGE, ATTN_W), lambda b, s, pt: (b, 0, 0)),
    ]
    in_specs += [page_spec(j) for j in range(pp)] + [page_spec(j) for j in range(pp)]
    in_specs += [
        pl.BlockSpec((4, QK_DIM), lambda b, s, pt: (0, 0)),
        pl.BlockSpec((1, V_DIM), lambda b, s, pt: (0, 0)),
    ]
    grid_spec = pltpu.PrefetchScalarGridSpec(
        num_scalar_prefetch=1,
        grid=(bd, n_pages // pp),
        in_specs=in_specs,
        out_specs=pl.BlockSpec((1, t_dec, ATTN_W), lambda b, s, pt: (b, 0, 0)),
        scratch_shapes=[
            pltpu.VMEM((rows, ATTN_W), BF16),
            pltpu.VMEM((rows, 1), F32),
            pltpu.VMEM((rows, 1), F32),
            pltpu.VMEM((rows, V_DIM), F32),
        ],
    )
    return pl.pallas_call(
        functools.partial(_sample_attn_kernel, pp=pp, t_dec=t_dec, past_len=float(n_pages * PAGE),
                          lam_init=lam_init),
        grid_spec=grid_spec,
        out_shape=jax.ShapeDtypeStruct((bd, t_dec, ATTN_W), BF16),
        compiler_params=_params(("parallel", "arbitrary")),
        name="attn_sample",
    )(page_table, q_s, kt_new, v_new, *([kt_cache] * pp), *([v_cache] * pp), lamp, g_subln)


def _seq_mixers(zf_ref, ucf_ref, t, z, u, gate_b, gate_c, wpool_ref, pscale, cw, ramp_cnt):
    zf_ref[HIST:HIST + t] = z
    uc = gate_c * u
    ucf_ref[HIST:HIST + t] = uc
    pooled_out = []
    for gi, w in enumerate(POOL_WINDOWS):
        cols = pl.ds(gi * POOL_GROUP, POOL_GROUP)
        zg = z[:, gi * POOL_GROUP:(gi + 1) * POOL_GROUP]
        acc = zg
        for j in range(1, w):
            acc = acc + zf_ref[pl.ds(HIST - j, t), cols]
        if ramp_cnt:
            pos = lax.broadcasted_iota(jnp.int32, (t, 1), 0)
            cnt = jnp.minimum(pos + 1, w).astype(F32)
            pooled = acc / cnt - zg
        else:
            pooled = acc * (1.0 / w) - zg
        pooled_out.append(jnp.dot(pooled.astype(BF16), wpool_ref[gi], preferred_element_type=F32))
    o_pool = jnp.concatenate(pooled_out, axis=-1) * pscale
    y = cw[0:1] * ucf_ref[pl.ds(HIST - 2, t), :] + cw[1:2] * ucf_ref[pl.ds(HIST - 1, t), :] + cw[2:3] * uc
    return o_pool, gate_b * y, uc


def _mix_project(x, o_attn, o_pool, o_conv, wo_ref, g_post):
    mix = jnp.dot(o_attn, wo_ref[0:ATTN_W], preferred_element_type=F32)
    mix += jnp.dot(o_pool.astype(BF16), wo_ref[ATTN_W:ATTN_W + POOL_W], preferred_element_type=F32)
    mix += jnp.dot(o_conv.astype(BF16), wo_ref[ATTN_W + POOL_W:], preferred_element_type=F32)
    return x + _rms(mix, g_post)


def _mix_main_kernel(x_ref, oa_ref, zpu_ref, gat_ref, hz_ref, hg_ref, wo_ref, wpool_ref, ps_ref, cw_ref,
                     g_ref, xo_ref, uct_ref, zf_ref, ucf_ref, *, tm, tiles_per_seq):
    i = pl.program_id(0)

    @pl.when(i % tiles_per_seq == 0)
    def _():
        hz = hz_ref[0]
        hg = hg_ref[0]
        zf_ref[0:HIST] = hz[:, :POOL_W]
        ucf_ref[0:HIST] = hg[:, CONV_CH:] * hz[:, POOL_W:]

    @pl.when(i % tiles_per_seq != 0)
    def _():
        zf_ref[0:HIST] = zf_ref[tm:tm + HIST]
        ucf_ref[0:HIST] = ucf_ref[tm:tm + HIST]

    zpu = zpu_ref[...]
    gat = gat_ref[...]
    o_pool, o_conv, uc = _seq_mixers(zf_ref, ucf_ref, tm, zpu[:, :POOL_W], zpu[:, POOL_W:], gat[:, :CONV_CH],
                                     gat[:, CONV_CH:], wpool_ref, ps_ref[...], cw_ref[...], False)
    uct_ref[0] = uc[tm - HIST:]
    xo_ref[...] = _mix_project(x_ref[...], oa_ref[...], o_pool, o_conv, wo_ref, g_ref[...])


def _mix_main(x, o_attn, zpu, gates, hist_zpu, hist_gates, wo, wpool, pscale, cw, g_post, *, tm, rows_per_seq,
              layer):
    n, d = x.shape
    tps = rows_per_seq // tm
    n_seq = n // rows_per_seq
    row = lambda i: (i, 0)
    const2 = lambda i: (0, 0)
    return pl.pallas_call(
        functools.partial(_mix_main_kernel, tm=tm, tiles_per_seq=tps),
        grid=(n // tm,),
        in_specs=[
            pl.BlockSpec((tm, d), row),
            pl.BlockSpec((tm, ATTN_W), row),
            pl.BlockSpec((tm, ATTN_W), row),
            pl.BlockSpec((tm, ATTN_W), row),
            pl.BlockSpec((1, HIST, ATTN_W), lambda i: (i // tps, 0, 0)),
            pl.BlockSpec((1, HIST, ATTN_W), lambda i: (i // tps, 0, 0)),
            _layer_weight(wo, layer),
            _layer_weight(wpool, layer),
            _resident(pscale.shape, const2),
            _resident(cw.shape, const2),
            _resident(g_post.shape, const2),
        ],
        out_specs=(
            pl.BlockSpec((tm, d), row),
            pl.BlockSpec((1, HIST, CONV_CH), lambda i: (i // tps, 0, 0)),
        ),
        out_shape=(
            jax.ShapeDtypeStruct((n, d), F32),
            jax.ShapeDtypeStruct((n_seq, HIST, CONV_CH), F32),
        ),
        scratch_shapes=[pltpu.VMEM((HIST + tm, POOL_W), F32), pltpu.VMEM((HIST + tm, CONV_CH), F32)],
        compiler_params=_params(("arbitrary",)),
        name="mix_main",
    )(x, o_attn, zpu, gates, hist_zpu, hist_gates, wo, wpool, pscale, cw, g_post)


def _mix_small_kernel(x_ref, oa_ref, zpu_ref, gat_ref, hp_ref, hc_ref, wo_ref, wpool_ref, ps_ref, cw_ref,
                      g_ref, xo_ref, uc_ref, zf_ref, ucf_ref, *, n_meta_seq, n_dec, t_dec):
    pools, convs, ucs = [], [], []
    segs = [(s * N_META, N_META, None) for s in range(n_meta_seq)]
    segs += [(n_meta_seq * N_META + b * t_dec, t_dec, b) for b in range(n_dec)]
    for start, t, b in segs:
        if b is None:
            zf_ref[0:HIST] = jnp.zeros((HIST, POOL_W), F32)
            ucf_ref[0:HIST] = jnp.zeros((HIST, CONV_CH), F32)
        else:
            zf_ref[0:HIST] = hp_ref[b]
            ucf_ref[0:HIST] = hc_ref[b]
        zpu = zpu_ref[start:start + t]
        gat = gat_ref[start:start + t]
        o_pool, o_conv, uc = _seq_mixers(zf_ref, ucf_ref, t, zpu[:, :POOL_W], zpu[:, POOL_W:], gat[:, :CONV_CH],
                                         gat[:, CONV_CH:], wpool_ref, ps_ref[...], cw_ref[...], b is None)
        pools.append(o_pool)
        convs.append(o_conv)
        ucs.append(uc)
    n_pad = x_ref.shape[0] - (n_meta_seq * N_META + n_dec * t_dec)
    if n_pad:
        pad = jnp.zeros((n_pad, POOL_W), F32)
        pools.append(pad)
        convs.append(pad)
        ucs.append(pad)
    uc_ref[...] = jnp.concatenate(ucs, axis=0)
    xo_ref[...] = _mix_project(x_ref[...], oa_ref[...], jnp.concatenate(pools, axis=0),
                               jnp.concatenate(convs, axis=0), wo_ref, g_ref[...])


def _mix_small(x, o_attn, zpu, gates, hist_pool, hist_conv, wo, wpool, pscale, cw, g_post, *, n_meta_seq,
               n_dec, t_dec, layer):
    n, d = x.shape
    whole = lambda a: pl.BlockSpec(a.shape, lambda i: (0,) * a.ndim)
    return pl.pallas_call(
        functools.partial(_mix_small_kernel, n_meta_seq=n_meta_seq, n_dec=n_dec, t_dec=t_dec),
        grid=(1,),
        in_specs=[whole(x), whole(o_attn), whole(zpu), whole(gates), whole(hist_pool), whole(hist_conv),
                  _layer_weight(wo, layer), _layer_weight(wpool, layer), whole(pscale), whole(cw), whole(g_post)],
        out_specs=(pl.BlockSpec((n, d), lambda i: (0, 0)), pl.BlockSpec((n, CONV_CH), lambda i: (0, 0))),
        out_shape=(jax.ShapeDtypeStruct((n, d), F32), jax.ShapeDtypeStruct((n, CONV_CH), F32)),
        scratch_shapes=[pltpu.VMEM((HIST + N_META, POOL_W), F32), pltpu.VMEM((HIST + N_META, CONV_CH), F32)],
        compiler_params=_params(("arbitrary",)),
        name="mix_small",
    )(x, o_attn, zpu, gates, hist_pool, hist_conv, wo, wpool, pscale, cw, g_post)


def _ffn_kernel(x_ref, gpre_ref, wg_ref, wu_ref, wd_ref, gpost_ref, o_ref, h_ref, acc_ref):
    c = pl.program_id(1)

    @pl.when(c == 0)
    def _():
        h_ref[...] = _rms(x_ref[...], gpre_ref[...]).astype(BF16)
        acc_ref[...] = jnp.zeros(acc_ref.shape, F32)

    h = h_ref[...]
    gate = jnp.dot(h, wg_ref[...], preferred_element_type=F32)
    up = jnp.dot(h, wu_ref[...], preferred_element_type=F32)
    act = (gate * jax.nn.sigmoid(gate) * up).astype(BF16)
    acc_ref[...] += jnp.dot(act, wd_ref[...], preferred_element_type=F32)

    @pl.when(c == pl.num_programs(1) - 1)
    def _():
        o_ref[...] = x_ref[...] + _rms(acc_ref[...], gpost_ref[...])


def _ffn(x, g_pre, w_gate_up, w_down, g_post, *, tm, tc, layer):
    n, d = x.shape
    hidden = w_down.shape[1]
    assert n % tm == 0 and hidden % tc == 0
    nc = hidden // tc
    return pl.pallas_call(
        _ffn_kernel,
        grid=(n // tm, nc),
        in_specs=[
            pl.BlockSpec((tm, d), lambda i, c: (i, 0)),
            pl.BlockSpec((1, d), lambda i, c: (0, 0)),
            pl.BlockSpec((None, d, tc), lambda i, c: (layer, 0, c)),
            pl.BlockSpec((None, d, tc), lambda i, c: (layer, 0, nc + c)),
            pl.BlockSpec((None, tc, d), lambda i, c: (layer, c, 0)),
            pl.BlockSpec((1, d), lambda i, c: (0, 0)),
        ],
        out_specs=pl.BlockSpec((tm, d), lambda i, c: (i, 0)),
        out_shape=jax.ShapeDtypeStruct((n, d), F32),
        scratch_shapes=[pltpu.VMEM((tm, d), BF16), pltpu.VMEM((tm, d), F32)],
        compiler_params=_params(("parallel", "arbitrary")),
        name="ffn",
    )(x, g_pre, w_gate_up, w_gate_up, w_down, g_post)


TQ = 256
TK = 256
TM_MIX = 512
TM_FFN = 512
TC_FFN = 512
PAGES_PER_STEP = 8


def kernel(x_prompt, x_sample, cache_k, cache_v, state_pool, state_conv, page_table, meta_tokens, w_in, w_o,
           lambda_q1, lambda_k1, lambda_q2, lambda_k2, g_subln, w_pool, pool_scale, conv_w, w_gate_up, w_down,
           g_pre_mix, g_post_mix, g_pre_ffn, g_post_ffn):
    nb, seq, d = x_prompt.shape
    bd, t_dec, _ = x_sample.shape
    depth = w_in.shape[0]
    n_pool = cache_k.shape[1]
    n_small = nb * N_META + bd * t_dec
    meta_rows = nb * N_META
    ns_pad = -(-n_small // PAGE) * PAGE

    kt_cache = jnp.transpose(cache_k, (0, 1, 3, 4, 5, 2)).reshape(depth, n_pool, ATTN_W, PAGE)
    v_cache = cache_v.reshape(depth, n_pool, PAGE * N_HEADS, V_DIM)

    xm = x_prompt.reshape(nb * seq, d)
    xs = jnp.concatenate([jnp.broadcast_to(meta_tokens[None], (nb, N_META, d)).reshape(meta_rows, d),
                          x_sample.reshape(bd * t_dec, d), jnp.zeros((ns_pad - n_small, d), F32)], axis=0)

    w_in_b, w_o_b, w_gu_b, w_dn_b, w_pool_b = (w.astype(BF16) for w in (w_in, w_o, w_gate_up, w_down, w_pool))

    outs = {k: [] for k in ("kpm", "kp", "vp", "pp", "cp", "ks", "vs", "ps", "cs")}
    for l in range(depth):
        lam_init = 0.8 - 0.6 * math.exp(-0.3 * l)
        lamp = jnp.stack([lambda_q1[l], lambda_k1[l], lambda_q2[l], lambda_k2[l]])
        g_sub = g_subln[l][None]
        g_col = g_subln[l][:, None]
        pscale = pool_scale[l][None]
        cw = conv_w[l]
        row = lambda a: a[l][None]

        qt_m, kb_m, ktf_m, vf_m, vtb_m, zpu_m, gat_m = _inproj(xm, row(g_pre_mix), w_in_b, TK, seq, l)
        qt_s, kb_s, ktf_s, vf_s, vtb_s, zpu_s, gat_s = _inproj(xs, row(g_pre_mix), w_in_b, ns_pad, ns_pad, l)
        ktf_s, vtb_s = ktf_s[0], vtb_s[0, 0]

        k_meta = kb_s[:meta_rows].reshape(nb, N_META, ATTN_W)
        vt_meta = vtb_s[:, :meta_rows].reshape(ATTN_W, nb, N_META).transpose(1, 0, 2)
        attn = functools.partial(_attention, lamp=lamp, g_col=g_col, n_seq=nb, lam_init=lam_init)
        oa_m = attn(qt_m, k_meta, vt_meta, kb_m, vtb_m, rows_per_seq=seq, pos_base0=N_META)
        qt_meta = jnp.pad(qt_s[:, :meta_rows].reshape(ATTN_W, nb, N_META), ((0, 0), (0, 0), (0, TQ - N_META)))
        oa_meta = attn(qt_meta.reshape(ATTN_W, nb * TQ), k_meta, vt_meta, None, None, rows_per_seq=TQ, pos_base0=0)
        oa_meta = oa_meta.reshape(nb, TQ, ATTN_W)[:, :N_META].reshape(meta_rows, ATTN_W)

        q_dec = qt_s[:, meta_rows:n_small].T.reshape(bd, t_dec, ATTN_W)
        kt_new = jnp.pad(ktf_s[:, meta_rows:n_small].astype(BF16).reshape(ATTN_W, bd, t_dec).transpose(1, 0, 2),
                         ((0, 0), (0, 0), (0, PAGE - t_dec)))
        v_new = jnp.pad(vf_s[meta_rows:n_small].astype(BF16).reshape(bd, t_dec, ATTN_W),
                        ((0, 0), (0, PAGE - t_dec), (0, 0)))
        oa_dec = _sample_attention(page_table, q_dec, kt_new, v_new, kt_cache, v_cache, lamp, g_sub, layer=l,
                                   pp=math.gcd(PAGES_PER_STEP, page_table.shape[1]), lam_init=lam_init)
        oa_s = jnp.concatenate([oa_meta, oa_dec.reshape(bd * t_dec, ATTN_W),
                                jnp.zeros((ns_pad - n_small, ATTN_W), BF16)], axis=0)

        hist_zpu = zpu_s[:meta_rows].reshape(nb, N_META, ATTN_W)
        hist_gat = gat_s[:meta_rows].reshape(nb, N_META, ATTN_W)
        xm, uc_tail = _mix_main(xm, oa_m, zpu_m, gat_m, hist_zpu, hist_gat, w_o_b, w_pool_b, pscale, cw,
                                row(g_post_mix), tm=min(TM_MIX, seq), rows_per_seq=seq, layer=l)
        hist_pool = jnp.pad(state_pool[l], ((0, 0), (HIST - state_pool.shape[2], 0), (0, 0)))
        hist_conv = jnp.pad(state_conv[l], ((0, 0), (HIST - state_conv.shape[2], 0), (0, 0)))
        xs, uc_s = _mix_small(xs, oa_s, zpu_s, gat_s, hist_pool, hist_conv, w_o_b, w_pool_b, pscale, cw,
                              row(g_post_mix), n_meta_seq=nb, n_dec=bd, t_dec=t_dec, layer=l)

        xm = _ffn(xm, row(g_pre_ffn), w_gu_b, w_dn_b, row(g_post_ffn), tm=min(TM_FFN, seq), tc=TC_FFN, layer=l)
        xs = _ffn(xs, row(g_pre_ffn), w_gu_b, w_dn_b, row(g_post_ffn), tm=ns_pad, tc=TC_FFN, layer=l)

        outs["kpm"].append(ktf_s[:, :meta_rows].reshape(ATTN_W, nb, N_META).transpose(1, 0, 2))
        outs["kp"].append(ktf_m)
        v_p = jnp.concatenate([vf_s[:meta_rows].reshape(nb, N_META, ATTN_W), vf_m.reshape(nb, seq, ATTN_W)], axis=1)
        outs["vp"].append(v_p.reshape(nb, N_META + seq, N_HEADS, V_DIM))
        z_m = zpu_m.reshape(nb, seq, ATTN_W)
        n_ph = state_pool.shape[2]
        n_ch = state_conv.shape[2]
        outs["pp"].append(z_m[:, seq - n_ph:, :POOL_W])
        outs["cp"].append(uc_tail[:, HIST - n_ch:])
        outs["ks"].append(ktf_s[:, meta_rows:n_small].T.reshape(bd, t_dec, N_HEADS, 2, QK_DIM))
        outs["vs"].append(vf_s[meta_rows:n_small].reshape(bd, t_dec, N_HEADS, V_DIM))
        z_dec = zpu_s[meta_rows:n_small, :POOL_W].reshape(bd, t_dec, POOL_W)
        outs["ps"].append(jnp.concatenate([state_pool[l], z_dec], axis=1)[:, -n_ph:])
        outs["cs"].append(uc_s[meta_rows:n_small].reshape(bd, t_dec, CONV_CH)[:, -n_ch:])

    st = lambda k: jnp.stack(outs[k])
    kt_p = jnp.concatenate([st("kpm"), st("kp")], axis=3)
    k_p = kt_p.reshape(depth, nb, N_HEADS, 2, QK_DIM, N_META + seq).transpose(0, 1, 5, 2, 3, 4)
    return (xm.reshape(nb, seq, d), xs[meta_rows:n_small].reshape(bd, t_dec, d),
            k_p, st("vp"), st("pp"), st("cp"), st("ks"), st("vs"), st("ps"), st("cs"))
```

```python
import functools
import math

import jax
import jax.numpy as jnp
import numpy as np
from jax import lax
from jax.experimental import pallas as pl
from jax.experimental.pallas import tpu as pltpu

N_META = 16
N_HEADS = 8
QK_DIM = 64
V_DIM = 128
HEAD_COLS = 2 * QK_DIM
ATTN_W = N_HEADS * V_DIM
POOL_W = 512
POOL_WINDOWS = (2, 4, 8, 16)
POOL_GROUP = 128
CONV_CH = 512
HIST = 16
EPS = 1e-6
NEG_INF = -1e30
LOG2E = 1.4426950408889634
PAGE = 128

F32 = jnp.float32
BF16 = jnp.bfloat16

VMEM_LIMIT_BYTES = 56 * 1024 * 1024


def _params(sem):
    return pltpu.CompilerParams(dimension_semantics=sem, vmem_limit_bytes=VMEM_LIMIT_BYTES)


def _resident(shape, index_map):
    return pl.BlockSpec(shape, index_map, pipeline_mode=pl.Buffered(1))


def _rms(x, g):
    r = lax.rsqrt(jnp.mean(x * x, axis=-1, keepdims=True) + EPS)
    return x * r * g


def _inproj_kernel(x_ref, g_ref, w_ref, qt_ref, kb_ref, ktf_ref, vf_ref, vtb_ref, zpu_ref, gat_ref):
    h = _rms(x_ref[...], g_ref[...]).astype(BF16)

    def proj(c):
        return jnp.dot(h, w_ref[:, c * ATTN_W:(c + 1) * ATTN_W], preferred_element_type=F32)

    qt_ref[...] = (proj(0) * (LOG2E * QK_DIM ** -0.5)).T.astype(BF16)
    k = proj(1)
    kb_ref[...] = k.astype(BF16)
    ktf_ref[0] = k.T
    v = proj(2)
    vf_ref[...] = v
    vtb_ref[0, 0] = v.T.astype(BF16)
    zpu_ref[...] = proj(3)
    gat_ref[...] = proj(4)


def _layer_weight(w, layer):
    return _resident((None,) + w.shape[1:], lambda *_: (layer,) + (0,) * (w.ndim - 1))


def _inproj(x, g, w, tm, rows_per_seq, layer):
    n, d = x.shape
    assert n % tm == 0 and rows_per_seq % tm == 0 and n % rows_per_seq == 0
    n_seq = n // rows_per_seq
    tps = rows_per_seq // tm
    row = lambda i: (i, 0)
    out_shape = (
        jax.ShapeDtypeStruct((ATTN_W, n), BF16),
        jax.ShapeDtypeStruct((n, ATTN_W), BF16),
        jax.ShapeDtypeStruct((n_seq, ATTN_W, rows_per_seq), F32),
        jax.ShapeDtypeStruct((n, ATTN_W), F32),
        jax.ShapeDtypeStruct((n_seq, tps, ATTN_W, tm), BF16),
        jax.ShapeDtypeStruct((n, ATTN_W), F32),
        jax.ShapeDtypeStruct((n, ATTN_W), F32),
    )
    out_specs = (
        pl.BlockSpec((ATTN_W, tm), lambda i: (0, i)),
        pl.BlockSpec((tm, ATTN_W), row),
        pl.BlockSpec((1, ATTN_W, tm), lambda i: (i // tps, 0, i % tps)),
        pl.BlockSpec((tm, ATTN_W), row),
        pl.BlockSpec((1, 1, ATTN_W, tm), lambda i: (i // tps, i % tps, 0, 0)),
        pl.BlockSpec((tm, ATTN_W), row),
        pl.BlockSpec((tm, ATTN_W), row),
    )
    return pl.pallas_call(
        _inproj_kernel,
        grid=(n // tm,),
        in_specs=[
            pl.BlockSpec((tm, d), row),
            _layer_weight(g, layer),
            _layer_weight(w, layer),
        ],
        out_specs=out_specs,
        out_shape=out_shape,
        compiler_params=_params(("parallel",)),
        name="inproj",
    )(x, g, w)


BIAS_ROWS = 16


def _bf16_parts(x):
    parts, rest = [], float(x)
    for _ in range(3):
        bits = np.array(rest, np.float32).view(np.uint32)
        bits = (bits + np.uint32(0x7FFF) + ((bits >> np.uint32(16)) & np.uint32(1))) & np.uint32(0xFFFF0000)
        part = float(bits.view(np.float32))
        parts.append(part)
        rest -= part
    return parts


def _lambda(lamp_ref, lam_init):
    lp = lamp_ref[...]
    s1 = jnp.sum(lp[0:1] * lp[1:2], axis=-1, keepdims=True)
    s2 = jnp.sum(lp[2:3] * lp[3:4], axis=-1, keepdims=True)
    return jnp.exp(s1) - jnp.exp(s2) + lam_init


def _attn_kernel(*refs, has_main, pos_base0, lam_init):
    if has_main:
        (qt_ref, km_ref, vtm_ref, k_ref, vt_ref, lamp_ref, g_ref, o_ref,
         q2t_ref, z_ref, zmax_ref, ka_ref, m_ref, l_ref, acc_ref) = refs
    else:
        qt_ref, km_ref, vtm_ref, lamp_ref, g_ref, o_ref, q2t_ref, z_ref, zmax_ref, m_ref, l_ref, acc_ref = refs
    qi = pl.program_id(1)
    w2 = 2 * TQ

    def heads(h):
        return slice(h * HEAD_COLS, (h + 1) * HEAD_COLS)

    def slope(h):
        return LOG2E * 2.0 ** -(h + 1)

    def q_local(shape):
        c = lax.broadcasted_iota(jnp.int32, shape, 1)
        return jnp.where(c >= TQ, c - TQ, c)

    row = lax.broadcasted_iota(jnp.int32, (HEAD_COLS, TQ), 0)
    if has_main:
        xrow = lax.broadcasted_iota(jnp.int32, (BIAS_ROWS, w2), 0)
        neg_q = -q_local((BIAS_ROWS, w2)).astype(F32)
        kcol = lax.broadcasted_iota(jnp.int32, (TK, HEAD_COLS), 1)
        kloc = lax.broadcasted_iota(jnp.int32, (TK, HEAD_COLS), 0).astype(F32)
    for h in range(N_HEADS):
        qth = qt_ref[heads(h), :]
        zero = jnp.zeros_like(qth)
        q2t_ref[h, 0:HEAD_COLS] = jnp.concatenate(
            [jnp.where(row < QK_DIM, qth, zero), jnp.where(row >= QK_DIM, qth, zero)], axis=1)
        if has_main:
            s1, s2, s3 = _bf16_parts(slope(h))
            ext = jnp.where(xrow == 0, s1, jnp.where(xrow == 1, s2, jnp.where(xrow == 2, s3,
                            jnp.where(xrow < 6, neg_q, 0.0))))
            q2t_ref[h, HEAD_COLS:HEAD_COLS + BIAS_ROWS] = ext.astype(BF16)
            q2t_ref[h, HEAD_COLS + BIAS_ROWS:] = jnp.zeros((HEAD_COLS - BIAS_ROWS, w2), BF16)
            ka = jnp.where(kcol < 3, kloc, jnp.where(kcol == 3, s1, jnp.where(kcol == 4, s2,
                           jnp.where(kcol == 5, s3, 0.0))))
            ka_ref[h] = ka.astype(BF16)
    m_ref[...] = jnp.full(m_ref.shape, NEG_INF, F32)
    l_ref[...] = jnp.zeros(l_ref.shape, F32)
    acc_ref[...] = jnp.zeros(acc_ref.shape, F32)

    def score_pass(h, width, score):
        z = score(h)
        z_ref[h, 0:width] = z
        zmax_ref[h] = jnp.max(z, axis=0, keepdims=True)

    def value_pass(h, width, vt_blk, c):
        m_old = m_ref[h]
        m_new = jnp.maximum(m_old, zmax_ref[h] + c(h))
        alpha = jnp.exp2(m_old - m_new)
        p = jnp.exp2(z_ref[h, 0:width] - (m_new - c(h))).astype(BF16)
        ones = jnp.ones((16, width), BF16)
        l_ref[h] = alpha * l_ref[h] + jnp.dot(ones, p, preferred_element_type=F32)[0:1]
        acc_ref[h] = alpha * acc_ref[h] + jnp.dot(vt_blk(h), p, preferred_element_type=F32)
        m_ref[h] = m_new

    shape_m = (N_META, w2)
    kj = lax.broadcasted_iota(jnp.int32, shape_m, 0)
    qpos = q_local(shape_m) + (pos_base0 + qi * TQ)
    vis_m = kj <= qpos
    dist_m = (qpos - kj).astype(F32)

    def meta_score(h):
        s = jnp.dot(km_ref[0, :, heads(h)], q2t_ref[h, 0:HEAD_COLS], preferred_element_type=F32)
        return s + jnp.where(vis_m, dist_m * -slope(h), NEG_INF)

    def key_block(width, score, vt_blk, c):
        for h in range(N_HEADS):
            score_pass(h, width, score)
        for h in range(N_HEADS):
            value_pass(h, width, vt_blk, c)

    key_block(N_META, meta_score, lambda h: vtm_ref[0, heads(h), :], lambda h: 0.0)

    if has_main:
        n_full = (qi * TQ) // TK

        def main_block(kb, masked):
            start = pl.multiple_of(kb * TK, TK)
            off = qi * TQ - kb * TK
            if masked:
                shape = (TK, w2)
                rel = q_local(shape) - lax.broadcasted_iota(jnp.int32, shape, 0)
                causal = (rel + off) >= 0

            def score(h):
                k_aug = jnp.concatenate([k_ref[pl.ds(start, TK), heads(h)], ka_ref[h]], axis=1)
                z = jnp.dot(k_aug, q2t_ref[h], preferred_element_type=F32)
                return jnp.where(causal, z, NEG_INF) if masked else z

            off_f = off.astype(F32)
            key_block(TK, score, lambda h: vt_ref[0, kb, heads(h), :], lambda h: off_f * -slope(h))

        def full_block(kb, carry):
            main_block(kb, False)
            return carry

        lax.fori_loop(0, n_full, full_block, 0)
        main_block(n_full, True)

    lam = _lambda(lamp_ref, lam_init)
    g_col = jnp.broadcast_to(g_ref[...], (V_DIM, TQ)) * (1.0 - lam_init)
    for h in range(N_HEADS):
        acc = acc_ref[h]
        l = l_ref[h]
        ot = acc[:, :TQ] / l[:, :TQ] - lam * (acc[:, TQ:] / l[:, TQ:])
        r = lax.rsqrt(jnp.mean(ot * ot, axis=0, keepdims=True) + EPS)
        o_ref[:, h * V_DIM:(h + 1) * V_DIM] = ((ot * r) * g_col).T.astype(BF16)


def _attention(qt, k_meta, vt_meta, k_main, vt_main, lamp, g_col, *, n_seq, rows_per_seq, pos_base0, lam_init,
               layer):
    nq = rows_per_seq // TQ
    has_main = k_main is not None
    in_specs = [
        pl.BlockSpec((ATTN_W, TQ), lambda b, i: (0, b * nq + i)),
        pl.BlockSpec((1, N_META, ATTN_W), lambda b, i: (b, 0, 0)),
        pl.BlockSpec((1, ATTN_W, N_META), lambda b, i: (b, 0, 0)),
    ]
    args = [qt, k_meta, vt_meta]
    scratch = [pltpu.VMEM((N_HEADS, 2 * HEAD_COLS if has_main else HEAD_COLS, 2 * TQ), BF16),
               pltpu.VMEM((N_HEADS, TK if has_main else N_META, 2 * TQ), F32),
               pltpu.VMEM((N_HEADS, 1, 2 * TQ), F32)]
    if has_main:
        assert rows_per_seq % TK == 0 and TK % TQ == 0
        in_specs += [
            pl.BlockSpec((rows_per_seq, ATTN_W), lambda b, i: (b, 0)),
            pl.BlockSpec((1, rows_per_seq // TK, ATTN_W, TK), lambda b, i: (b, 0, 0, 0)),
        ]
        args += [k_main, vt_main]
        scratch += [pltpu.VMEM((N_HEADS, TK, HEAD_COLS), BF16)]
    in_specs += [_layer_weight(lamp, layer), _layer_weight(g_col, layer)]
    args += [lamp, g_col]
    scratch += [
        pltpu.VMEM((N_HEADS, 1, 2 * TQ), F32),
        pltpu.VMEM((N_HEADS, 1, 2 * TQ), F32),
        pltpu.VMEM((N_HEADS, V_DIM, 2 * TQ), F32),
    ]
    return pl.pallas_call(
        functools.partial(_attn_kernel, has_main=has_main, pos_base0=pos_base0, lam_init=lam_init),
        grid=(n_seq, nq),
        in_specs=in_specs,
        out_specs=pl.BlockSpec((TQ, ATTN_W), lambda b, i: (b * nq + i, 0)),
        out_shape=jax.ShapeDtypeStruct((n_seq * rows_per_seq, ATTN_W), BF16),
        scratch_shapes=scratch,
        compiler_params=_params(("parallel", "parallel")),
        name="attn_prompt" if has_main else "attn_meta",
    )(*args)


def _sample_attn_kernel(pt_ref, q_ref, ktn_ref, vn_ref, *rest, pp, t_dec, past_len, lam_init):
    kt_refs = rest[:pp]
    v_refs = rest[pp:2 * pp]
    lamp_ref, g_ref, o_ref, qbd_ref, m_ref, l_ref, acc_ref = rest[2 * pp:]
    s_idx = pl.program_id(1)
    rows = N_HEADS * 2 * t_dec
    grp = 2 * t_dec

    r_col = lax.broadcasted_iota(jnp.int32, (rows, 1), 0)
    head = r_col // grp
    t_of_r = (r_col % t_dec).astype(F32)
    slope = lax.bitcast_convert_type((126 - head) << 23, F32) * LOG2E

    def pv_update(z, v_of_head):
        m_old = m_ref[...]
        m_new = jnp.maximum(m_old, jnp.max(z, axis=-1, keepdims=True))
        alpha = jnp.exp2(m_old - m_new)
        p = jnp.exp2(z - m_new)
        l_ref[...] = alpha * l_ref[...] + jnp.sum(p, axis=-1, keepdims=True)
        pb = p.astype(BF16)
        pv = [jnp.dot(pb[hh * grp:(hh + 1) * grp], v_of_head(hh), preferred_element_type=F32)
              for hh in range(N_HEADS)]
        acc_ref[...] = alpha * acc_ref[...] + jnp.concatenate(pv, axis=0)
        m_ref[...] = m_new

    @pl.when(s_idx == 0)
    def _init():
        qf = q_ref[0].astype(F32)
        qt = jnp.concatenate([qf] * (2 * N_HEADS), axis=0)
        rr = lax.broadcasted_iota(jnp.int32, qt.shape, 0) // t_dec
        cc = lax.broadcasted_iota(jnp.int32, qt.shape, 1) // QK_DIM
        qbd_ref[...] = jnp.where(rr == cc, qt, 0.0).astype(BF16)
        m_ref[...] = jnp.full(m_ref.shape, NEG_INF, F32)
        l_ref[...] = jnp.zeros(l_ref.shape, F32)
        acc_ref[...] = jnp.zeros(acc_ref.shape, F32)
        s = jnp.dot(qbd_ref[...], ktn_ref[0], preferred_element_type=F32)
        kj = lax.broadcasted_iota(jnp.int32, s.shape, 1)
        tq_i = lax.broadcasted_iota(jnp.int32, s.shape, 0) % t_dec
        dist = (tq_i - kj).astype(F32)
        z = jnp.where((kj < t_dec) & (kj <= tq_i), s - slope * dist, NEG_INF)
        vn = vn_ref[0]
        pv_update(z, lambda hh: vn[:, hh * V_DIM:(hh + 1) * V_DIM])

    qbd = qbd_ref[...]
    s = jnp.concatenate(
        [jnp.dot(qbd, kt_refs[j][0, 0].astype(BF16), preferred_element_type=F32) for j in range(pp)],
        axis=-1)
    col = lax.broadcasted_iota(jnp.int32, s.shape, 1).astype(F32)
    base = (s_idx * (pp * PAGE)).astype(F32)
    dist = (past_len + t_of_r - base) - col
    z = s - slope * dist

    def v_cached(hh):
        return jnp.concatenate(
            [v_refs[j][0, 0, pl.ds(hh, PAGE, stride=N_HEADS), :].astype(BF16) for j in range(pp)], axis=0)

    pv_update(z, v_cached)

    @pl.when(s_idx == pl.num_programs(1) - 1)
    def _fin():
        lam = _lambda(lamp_ref, lam_init)
        acc = acc_ref[...]
        l = l_ref[...]
        g = g_ref[...]
        outs = []
        for hh in range(N_HEADS):
            a0, a1 = hh * grp, hh * grp + t_dec
            o = acc[a0:a0 + t_dec] / l[a0:a0 + t_dec] - lam * (acc[a1:a1 + t_dec] / l[a1:a1 + t_dec])
            outs.append(_rms(o, g) * (1.0 - lam_init))
        o_ref[0] = jnp.concatenate(outs, axis=-1).astype(BF16)


def _sample_attention(page_table, q_s, kt_new, v_new, kt_cache, v_cache, lamp, g_subln, *, layer, pp,
                      lam_init):
    bd, t_dec, _ = q_s.shape
    n_pages = page_table.shape[1]
    assert n_pages % pp == 0
    rows = N_HEADS * 2 * t_dec

    def page_spec(j):
        return pl.BlockSpec((1, 1, ATTN_W, PAGE), lambda b, s, pt: (layer, pt[b, s * pp + j], 0, 0))

    in_specs = [
        pl.BlockSpec((1, t_dec, ATTN_W), lambda b, s, pt: (b, 0, 0)),
        pl.BlockSpec((1, ATTN_W, PAGE), lambda b, s, pt: (b, 0, 0)),
        pl.BlockSpec((1, PAGE, ATTN_W), lambda b, s, pt: (b, 0, 0)),
    ]
    in_specs += [page_spec(j) for j in range(pp)] + [page_spec(j) for j in range(pp)]
    in_specs += [_layer_weight(lamp, layer), _layer_weight(g_subln, layer)]
    grid_spec = pltpu.PrefetchScalarGridSpec(
        num_scalar_prefetch=1,
        grid=(bd, n_pages // pp),
        in_specs=in_specs,
        out_specs=pl.BlockSpec((1, t_dec, ATTN_W), lambda b, s, pt: (b, 0, 0)),
        scratch_shapes=[
            pltpu.VMEM((rows, ATTN_W), BF16),
            pltpu.VMEM((rows, 1), F32),
            pltpu.VMEM((rows, 1), F32),
            pltpu.VMEM((rows, V_DIM), F32),
        ],
    )
    return pl.pallas_call(
        functools.partial(_sample_attn_kernel, pp=pp, t_dec=t_dec, past_len=float(n_pages * PAGE),
                          lam_init=lam_init),
        grid_spec=grid_spec,
        out_shape=jax.ShapeDtypeStruct((bd, t_dec, ATTN_W), BF16),
        compiler_params=_params(("parallel", "arbitrary")),
        name="attn_sample",
    )(page_table, q_s, kt_new, v_new, *([kt_cache] * pp), *([v_cache] * pp), lamp, g_subln)


def _seq_mixers(zf_ref, ucf_ref, t, z, u, gate_b, gate_c, wpool_ref, pscale, cw, ramp_cnt):
    zf_ref[HIST:HIST + t] = z
    uc = gate_c * u
    ucf_ref[HIST:HIST + t] = uc
    pooled_out = []
    for gi, w in enumerate(POOL_WINDOWS):
        cols = pl.ds(gi * POOL_GROUP, POOL_GROUP)
        zg = z[:, gi * POOL_GROUP:(gi + 1) * POOL_GROUP]
        acc = zg
        for j in range(1, w):
            acc = acc + zf_ref[pl.ds(HIST - j, t), cols]
        if ramp_cnt:
            pos = lax.broadcasted_iota(jnp.int32, (t, 1), 0)
            cnt = jnp.minimum(pos + 1, w).astype(F32)
            pooled = acc / cnt - zg
        else:
            pooled = acc * (1.0 / w) - zg
        pooled_out.append(jnp.dot(pooled.astype(BF16), wpool_ref[gi], preferred_element_type=F32))
    o_pool = jnp.concatenate(pooled_out, axis=-1) * pscale
    y = cw[0:1] * ucf_ref[pl.ds(HIST - 2, t), :] + cw[1:2] * ucf_ref[pl.ds(HIST - 1, t), :] + cw[2:3] * uc
    return o_pool, gate_b * y, uc


def _mix_project(x, o_attn, o_pool, o_conv, wo_ref, g_post):
    mix = jnp.dot(o_attn, wo_ref[0:ATTN_W], preferred_element_type=F32)
    mix += jnp.dot(o_pool.astype(BF16), wo_ref[ATTN_W:ATTN_W + POOL_W], preferred_element_type=F32)
    mix += jnp.dot(o_conv.astype(BF16), wo_ref[ATTN_W + POOL_W:], preferred_element_type=F32)
    return x + _rms(mix, g_post)


def _mix_main_kernel(x_ref, oa_ref, zpu_ref, gat_ref, hz_ref, hg_ref, wo_ref, wpool_ref, ps_ref, cw_ref,
                     g_ref, xo_ref, uct_ref, zf_ref, ucf_ref, *, tm, tiles_per_seq):
    i = pl.program_id(0)

    @pl.when(i % tiles_per_seq == 0)
    def _():
        hz = hz_ref[...]
        hg = hg_ref[...]
        zf_ref[0:HIST] = hz[:, :POOL_W]
        ucf_ref[0:HIST] = hg[:, CONV_CH:] * hz[:, POOL_W:]

    @pl.when(i % tiles_per_seq != 0)
    def _():
        zf_ref[0:HIST] = zf_ref[tm:tm + HIST]
        ucf_ref[0:HIST] = ucf_ref[tm:tm + HIST]

    zpu = zpu_ref[...]
    gat = gat_ref[...]
    o_pool, o_conv, uc = _seq_mixers(zf_ref, ucf_ref, tm, zpu[:, :POOL_W], zpu[:, POOL_W:], gat[:, :CONV_CH],
                                     gat[:, CONV_CH:], wpool_ref, ps_ref[...], cw_ref[...], False)
    uct_ref[0] = uc[tm - HIST:]
    xo_ref[...] = _mix_project(x_ref[...], oa_ref[...], o_pool, o_conv, wo_ref, g_ref[...])


def _mix_main(x, o_attn, zpu, gates, hist_zpu, hist_gates, wo, wpool, pscale, cw, g_post, *, tm, rows_per_seq,
              layer):
    n, d = x.shape
    tps = rows_per_seq // tm
    n_seq = n // rows_per_seq
    row = lambda i: (i, 0)
    const2 = lambda i: (0, 0)
    return pl.pallas_call(
        functools.partial(_mix_main_kernel, tm=tm, tiles_per_seq=tps),
        grid=(n // tm,),
        in_specs=[
            pl.BlockSpec((tm, d), row),
            pl.BlockSpec((tm, ATTN_W), row),
            pl.BlockSpec((tm, ATTN_W), row),
            pl.BlockSpec((tm, ATTN_W), row),
            pl.BlockSpec((HIST, ATTN_W), lambda i: (i // tps, 0)),
            pl.BlockSpec((HIST, ATTN_W), lambda i: (i // tps, 0)),
            _layer_weight(wo, layer),
            _layer_weight(wpool, layer),
            _layer_weight(pscale, layer),
            _layer_weight(cw, layer),
            _layer_weight(g_post, layer),
        ],
        out_specs=(
            pl.BlockSpec((tm, d), row),
            pl.BlockSpec((1, HIST, CONV_CH), lambda i: (i // tps, 0, 0)),
        ),
        out_shape=(
            jax.ShapeDtypeStruct((n, d), F32),
            jax.ShapeDtypeStruct((n_seq, HIST, CONV_CH), F32),
        ),
        scratch_shapes=[pltpu.VMEM((HIST + tm, POOL_W), F32), pltpu.VMEM((HIST + tm, CONV_CH), F32)],
        compiler_params=_params(("arbitrary",)),
        name="mix_main",
    )(x, o_attn, zpu, gates, hist_zpu, hist_gates, wo, wpool, pscale, cw, g_post)


def _mix_small_kernel(x_ref, oa_ref, zpu_ref, gat_ref, hp_ref, hc_ref, wo_ref, wpool_ref, ps_ref, cw_ref,
                      g_ref, xo_ref, uc_ref, zf_ref, ucf_ref, *, n_meta_seq, n_dec, t_dec):
    pools, convs, ucs = [], [], []
    segs = [(s * N_META, N_META, None) for s in range(n_meta_seq)]
    segs += [(n_meta_seq * N_META + b * t_dec, t_dec, b) for b in range(n_dec)]
    for start, t, b in segs:
        if b is None:
            zf_ref[0:HIST] = jnp.zeros((HIST, POOL_W), F32)
            ucf_ref[0:HIST] = jnp.zeros((HIST, CONV_CH), F32)
        else:
            zf_ref[0:HIST] = hp_ref[b]
            ucf_ref[0:HIST] = hc_ref[b]
        zpu = zpu_ref[start:start + t]
        gat = gat_ref[start:start + t]
        o_pool, o_conv, uc = _seq_mixers(zf_ref, ucf_ref, t, zpu[:, :POOL_W], zpu[:, POOL_W:], gat[:, :CONV_CH],
                                         gat[:, CONV_CH:], wpool_ref, ps_ref[...], cw_ref[...], b is None)
        pools.append(o_pool)
        convs.append(o_conv)
        ucs.append(uc)
    n_pad = x_ref.shape[0] - (n_meta_seq * N_META + n_dec * t_dec)
    if n_pad:
        pad = jnp.zeros((n_pad, POOL_W), F32)
        pools.append(pad)
        convs.append(pad)
        ucs.append(pad)
    uc_ref[...] = jnp.concatenate(ucs, axis=0)
    xo_ref[...] = _mix_project(x_ref[...], oa_ref[...], jnp.concatenate(pools, axis=0),
                               jnp.concatenate(convs, axis=0), wo_ref, g_ref[...])


def _mix_small(x, o_attn, zpu, gates, hist_pool, hist_conv, wo, wpool, pscale, cw, g_post, *, n_meta_seq,
               n_dec, t_dec, layer):
    n, d = x.shape
    whole = lambda a: pl.BlockSpec(a.shape, lambda i: (0,) * a.ndim)
    return pl.pallas_call(
        functools.partial(_mix_small_kernel, n_meta_seq=n_meta_seq, n_dec=n_dec, t_dec=t_dec),
        grid=(1,),
        in_specs=[whole(x), whole(o_attn), whole(zpu), whole(gates),
                  _layer_weight(hist_pool, layer), _layer_weight(hist_conv, layer),
                  _layer_weight(wo, layer), _layer_weight(wpool, layer), _layer_weight(pscale, layer), _layer_weight(cw, layer), _layer_weight(g_post, layer)],
        out_specs=(pl.BlockSpec((n, d), lambda i: (0, 0)), pl.BlockSpec((n, CONV_CH), lambda i: (0, 0))),
        out_shape=(jax.ShapeDtypeStruct((n, d), F32), jax.ShapeDtypeStruct((n, CONV_CH), F32)),
        scratch_shapes=[pltpu.VMEM((HIST + N_META, POOL_W), F32), pltpu.VMEM((HIST + N_META, CONV_CH), F32)],
        compiler_params=_params(("arbitrary",)),
        name="mix_small",
    )(x, o_attn, zpu, gates, hist_pool, hist_conv, wo, wpool, pscale, cw, g_post)


def _ffn_kernel(x_ref, gpre_ref, wg_ref, wu_ref, wd_ref, gpost_ref, o_ref, h_ref, acc_ref):
    c = pl.program_id(1)

    @pl.when(c == 0)
    def _():
        h_ref[...] = _rms(x_ref[...], gpre_ref[...]).astype(BF16)
        acc_ref[...] = jnp.zeros(acc_ref.shape, F32)

    h = h_ref[...]
    gate = jnp.dot(h, wg_ref[...], preferred_element_type=F32)
    up = jnp.dot(h, wu_ref[...], preferred_element_type=F32)
    act = (gate * jax.nn.sigmoid(gate) * up).astype(BF16)
    acc_ref[...] += jnp.dot(act, wd_ref[...], preferred_element_type=F32)

    @pl.when(c == pl.num_programs(1) - 1)
    def _():
        o_ref[...] = x_ref[...] + _rms(acc_ref[...], gpost_ref[...])


def _ffn(x, g_pre, w_gate_up, w_down, g_post, *, tm, tc, layer):
    n, d = x.shape
    hidden = w_down.shape[1]
    assert n % tm == 0 and hidden % tc == 0
    nc = hidden // tc
    return pl.pallas_call(
        _ffn_kernel,
        grid=(n // tm, nc),
        in_specs=[
            pl.BlockSpec((tm, d), lambda i, c: (i, 0)),
            _layer_weight(g_pre, layer),
            pl.BlockSpec((None, d, tc), lambda i, c: (layer, 0, c)),
            pl.BlockSpec((None, d, tc), lambda i, c: (layer, 0, nc + c)),
            pl.BlockSpec((None, tc, d), lambda i, c: (layer, c, 0)),
            _layer_weight(g_post, layer),
        ],
        out_specs=pl.BlockSpec((tm, d), lambda i, c: (i, 0)),
        out_shape=jax.ShapeDtypeStruct((n, d), F32),
        scratch_shapes=[pltpu.VMEM((tm, d), BF16), pltpu.VMEM((tm, d), F32)],
        compiler_params=_params(("parallel", "arbitrary")),
        name="ffn",
    )(x, g_pre, w_gate_up, w_gate_up, w_down, g_post)


TQ = 256
TK = 256
TM_MIX = 512
TM_FFN = 512
TC_FFN = 512
PAGES_PER_STEP = 16


def kernel(x_prompt, x_sample, cache_k, cache_v, state_pool, state_conv, page_table, meta_tokens, w_in, w_o,
           lambda_q1, lambda_k1, lambda_q2, lambda_k2, g_subln, w_pool, pool_scale, conv_w, w_gate_up, w_down,
           g_pre_mix, g_post_mix, g_pre_ffn, g_post_ffn):
    nb, seq, d = x_prompt.shape
    bd, t_dec, _ = x_sample.shape
    depth = w_in.shape[0]
    n_pool = cache_k.shape[1]
    n_small = nb * N_META + bd * t_dec
    meta_rows = nb * N_META
    ns_pad = -(-n_small // PAGE) * PAGE

    kt_cache = jnp.transpose(cache_k, (0, 1, 3, 4, 5, 2)).reshape(depth, n_pool, ATTN_W, PAGE)
    v_cache = cache_v.reshape(depth, n_pool, PAGE * N_HEADS, V_DIM)

    xm = x_prompt.reshape(nb * seq, d)
    xs = jnp.concatenate([jnp.broadcast_to(meta_tokens[None], (nb, N_META, d)).reshape(meta_rows, d),
                          x_sample.reshape(bd * t_dec, d), jnp.zeros((ns_pad - n_small, d), F32)], axis=0)

    w_in_b, w_o_b, w_gu_b, w_dn_b, w_pool_b = (w.astype(BF16) for w in (w_in, w_o, w_gate_up, w_down, w_pool))

    lamp = jnp.stack([lambda_q1, lambda_k1, lambda_q2, lambda_k2], axis=1)
    row = lambda a: a[:, None, :]
    g_sub = row(g_subln)
    g_col = g_subln[:, :, None]
    pscale = row(pool_scale)
    cw = conv_w
    g_pre_mix, g_post_mix, g_pre_ffn, g_post_ffn = (row(g) for g in (g_pre_mix, g_post_mix, g_pre_ffn, g_post_ffn))
    hist_pool = jnp.pad(state_pool, ((0, 0), (0, 0), (HIST - state_pool.shape[2], 0), (0, 0)))
    hist_conv = jnp.pad(state_conv, ((0, 0), (0, 0), (HIST - state_conv.shape[2], 0), (0, 0)))

    outs = {k: [] for k in ("kpm", "kp", "vp", "pp", "cp", "ks", "vs", "ps", "cs")}
    for l in range(depth):
        lam_init = 0.8 - 0.6 * math.exp(-0.3 * l)

        qt_m, kb_m, ktf_m, vf_m, vtb_m, zpu_m, gat_m = _inproj(xm, g_pre_mix, w_in_b, TK, seq, l)
        qt_s, kb_s, ktf_s, vf_s, vtb_s, zpu_s, gat_s = _inproj(xs, g_pre_mix, w_in_b, ns_pad, ns_pad, l)
        ktf_s, vtb_s = ktf_s[0], vtb_s[0, 0]

        k_meta = kb_s[:meta_rows].reshape(nb, N_META, ATTN_W)
        vt_meta = vtb_s[:, :meta_rows].reshape(ATTN_W, nb, N_META).transpose(1, 0, 2)
        attn = functools.partial(_attention, lamp=lamp, g_col=g_col, n_seq=nb, lam_init=lam_init, layer=l)
        oa_m = attn(qt_m, k_meta, vt_meta, kb_m, vtb_m, rows_per_seq=seq, pos_base0=N_META)
        qt_meta = jnp.pad(qt_s[:, :meta_rows].reshape(ATTN_W, nb, N_META), ((0, 0), (0, 0), (0, TQ - N_META)))
        oa_meta = attn(qt_meta.reshape(ATTN_W, nb * TQ), k_meta, vt_meta, None, None, rows_per_seq=TQ, pos_base0=0)
        oa_meta = oa_meta.reshape(nb, TQ, ATTN_W)[:, :N_META].reshape(meta_rows, ATTN_W)

        q_dec = qt_s[:, meta_rows:n_small].T.reshape(bd, t_dec, ATTN_W)
        kt_new = jnp.pad(ktf_s[:, meta_rows:n_small].astype(BF16).reshape(ATTN_W, bd, t_dec).transpose(1, 0, 2),
                         ((0, 0), (0, 0), (0, PAGE - t_dec)))
        v_new = jnp.pad(vf_s[meta_rows:n_small].astype(BF16).reshape(bd, t_dec, ATTN_W),
                        ((0, 0), (0, PAGE - t_dec), (0, 0)))
        oa_dec = _sample_attention(page_table, q_dec, kt_new, v_new, kt_cache, v_cache, lamp, g_sub, layer=l,
                                   pp=math.gcd(PAGES_PER_STEP, page_table.shape[1]), lam_init=lam_init)
        oa_s = jnp.concatenate([oa_meta, oa_dec.reshape(bd * t_dec, ATTN_W),
                                jnp.zeros((ns_pad - n_small, ATTN_W), BF16)], axis=0)

        assert N_META == HIST
        xm, uc_tail = _mix_main(xm, oa_m, zpu_m, gat_m, zpu_s, gat_s, w_o_b, w_pool_b, pscale, cw,
                                g_post_mix, tm=min(TM_MIX, seq), rows_per_seq=seq, layer=l)
        xs, uc_s = _mix_small(xs, oa_s, zpu_s, gat_s, hist_pool, hist_conv, w_o_b, w_pool_b, pscale, cw,
                              g_post_mix, n_meta_seq=nb, n_dec=bd, t_dec=t_dec, layer=l)

        xm = _ffn(xm, g_pre_ffn, w_gu_b, w_dn_b, g_post_ffn, tm=min(TM_FFN, seq), tc=TC_FFN, layer=l)
        xs = _ffn(xs, g_pre_ffn, w_gu_b, w_dn_b, g_post_ffn, tm=ns_pad, tc=TC_FFN, layer=l)

        outs["kpm"].append(ktf_s[:, :meta_rows].reshape(ATTN_W, nb, N_META).transpose(1, 0, 2))
        outs["kp"].append(ktf_m)
        v_p = jnp.concatenate([vf_s[:meta_rows].reshape(nb, N_META, ATTN_W), vf_m.reshape(nb, seq, ATTN_W)], axis=1)
        outs["vp"].append(v_p.reshape(nb, N_META + seq, N_HEADS, V_DIM))
        z_m = zpu_m.reshape(nb, seq, ATTN_W)
        n_ph = state_pool.shape[2]
        n_ch = state_conv.shape[2]
        outs["pp"].append(z_m[:, seq - n_ph:, :POOL_W])
        outs["cp"].append(uc_tail[:, HIST - n_ch:])
        outs["ks"].append(ktf_s[:, meta_rows:n_small].T.reshape(bd, t_dec, N_HEADS, 2, QK_DIM))
        outs["vs"].append(vf_s[meta_rows:n_small].reshape(bd, t_dec, N_HEADS, V_DIM))
        z_dec = zpu_s[meta_rows:n_small, :POOL_W].reshape(bd, t_dec, POOL_W)
        outs["ps"].append(jnp.concatenate([state_pool[l], z_dec], axis=1)[:, -n_ph:])
        outs["cs"].append(uc_s[meta_rows:n_small].reshape(bd, t_dec, CONV_CH)[:, -n_ch:])

    st = lambda k: jnp.stack(outs[k])
    kt_p = jnp.concatenate([st("kpm"), st("kp")], axis=3)
    k_p = kt_p.reshape(depth, nb, N_HEADS, 2, QK_DIM, N_META + seq).transpose(0, 1, 5, 2, 3, 4)
    return (xm.reshape(nb, seq, d), xs[meta_rows:n_small].reshape(bd, t_dec, d),
            k_p, st("vp"), st("pp"), st("cp"), st("ks"), st("vs"), st("ps"), st("cs"))
```

```python
import functools
import math

import jax
import jax.numpy as jnp
import numpy as np
from jax import lax
from jax.experimental import pallas as pl
from jax.experimental.pallas import tpu as pltpu

N_META = 16
N_HEADS = 8
QK_DIM = 64
V_DIM = 128
HEAD_COLS = 2 * QK_DIM
ATTN_W = N_HEADS * V_DIM
POOL_W = 512
POOL_WINDOWS = (2, 4, 8, 16)
POOL_GROUP = 128
CONV_CH = 512
HIST = 16
EPS = 1e-6
NEG_INF = -1e30
LOG2E = 1.4426950408889634
PAGE = 128

F32 = jnp.float32
BF16 = jnp.bfloat16

VMEM_LIMIT_BYTES = 56 * 1024 * 1024


def _params(sem):
    return pltpu.CompilerParams(dimension_semantics=sem, vmem_limit_bytes=VMEM_LIMIT_BYTES)


def _resident(shape, index_map):
    return pl.BlockSpec(shape, index_map, pipeline_mode=pl.Buffered(1))


def _rms(x, g):
    r = lax.rsqrt(jnp.mean(x * x, axis=-1, keepdims=True) + EPS)
    return x * r * g


def _inproj_kernel(x_ref, g_ref, w_ref, qt_ref, kb_ref, ktf_ref, vf_ref, vtb_ref, zpu_ref, gat_ref):
    h = _rms(x_ref[...], g_ref[...]).astype(BF16)

    def proj(c):
        return jnp.dot(h, w_ref[:, c * ATTN_W:(c + 1) * ATTN_W], preferred_element_type=F32)

    qt_ref[...] = (proj(0) * (LOG2E * QK_DIM ** -0.5)).T.astype(BF16)
    k = proj(1)
    kb_ref[...] = k.astype(BF16)
    ktf_ref[0] = k.T
    v = proj(2)
    vf_ref[...] = v
    vtb_ref[0, 0] = v.T.astype(BF16)
    zpu_ref[...] = proj(3)
    gat_ref[...] = proj(4)


def _layer_weight(w, layer):
    return _resident((None,) + w.shape[1:], lambda *_: (layer,) + (0,) * (w.ndim - 1))


def _inproj(x, g, w, tm, rows_per_seq, layer):
    n, d = x.shape
    assert n % tm == 0 and rows_per_seq % tm == 0 and n % rows_per_seq == 0
    n_seq = n // rows_per_seq
    tps = rows_per_seq // tm
    row = lambda i: (i, 0)
    out_shape = (
        jax.ShapeDtypeStruct((ATTN_W, n), BF16),
        jax.ShapeDtypeStruct((n, ATTN_W), BF16),
        jax.ShapeDtypeStruct((n_seq, ATTN_W, rows_per_seq), F32),
        jax.ShapeDtypeStruct((n, ATTN_W), F32),
        jax.ShapeDtypeStruct((n_seq, tps, ATTN_W, tm), BF16),
        jax.ShapeDtypeStruct((n, ATTN_W), F32),
        jax.ShapeDtypeStruct((n, ATTN_W), F32),
    )
    out_specs = (
        pl.BlockSpec((ATTN_W, tm), lambda i: (0, i)),
        pl.BlockSpec((tm, ATTN_W), row),
        pl.BlockSpec((1, ATTN_W, tm), lambda i: (i // tps, 0, i % tps)),
        pl.BlockSpec((tm, ATTN_W), row),
        pl.BlockSpec((1, 1, ATTN_W, tm), lambda i: (i // tps, i % tps, 0, 0)),
        pl.BlockSpec((tm, ATTN_W), row),
        pl.BlockSpec((tm, ATTN_W), row),
    )
    return pl.pallas_call(
        _inproj_kernel,
        grid=(n // tm,),
        in_specs=[
            pl.BlockSpec((tm, d), row),
            _layer_weight(g, layer),
            _layer_weight(w, layer),
        ],
        out_specs=out_specs,
        out_shape=out_shape,
        compiler_params=_params(("parallel",)),
        name="inproj",
    )(x, g, w)


BIAS_ROWS = 16


def _bf16_parts(x):
    parts, rest = [], float(x)
    for _ in range(3):
        bits = np.array(rest, np.float32).view(np.uint32)
        bits = (bits + np.uint32(0x7FFF) + ((bits >> np.uint32(16)) & np.uint32(1))) & np.uint32(0xFFFF0000)
        part = float(bits.view(np.float32))
        parts.append(part)
        rest -= part
    return parts


def _lambda(lamp_ref, lam_init):
    lp = lamp_ref[...]
    s1 = jnp.sum(lp[0:1] * lp[1:2], axis=-1, keepdims=True)
    s2 = jnp.sum(lp[2:3] * lp[3:4], axis=-1, keepdims=True)
    return jnp.exp(s1) - jnp.exp(s2) + lam_init


def _attn_kernel(*refs, has_main, pos_base0, lam_init):
    if has_main:
        (qt_ref, km_ref, vtm_ref, k_ref, vt_ref, lamp_ref, g_ref, o_ref,
         q2t_ref, z_ref, zmax_ref, ka_ref, m_ref, l_ref, acc_ref) = refs
    else:
        qt_ref, km_ref, vtm_ref, lamp_ref, g_ref, o_ref, q2t_ref, z_ref, zmax_ref, m_ref, l_ref, acc_ref = refs
    qi = pl.program_id(1)
    w2 = 2 * TQ

    def heads(h):
        return slice(h * HEAD_COLS, (h + 1) * HEAD_COLS)

    def slope(h):
        return LOG2E * 2.0 ** -(h + 1)

    def q_local(shape):
        c = lax.broadcasted_iota(jnp.int32, shape, 1)
        return jnp.where(c >= TQ, c - TQ, c)

    row = lax.broadcasted_iota(jnp.int32, (HEAD_COLS, TQ), 0)
    if has_main:
        xrow = lax.broadcasted_iota(jnp.int32, (BIAS_ROWS, w2), 0)
        neg_q = -q_local((BIAS_ROWS, w2)).astype(F32)
        kcol = lax.broadcasted_iota(jnp.int32, (TK, HEAD_COLS), 1)
        kloc = lax.broadcasted_iota(jnp.int32, (TK, HEAD_COLS), 0).astype(F32)
    for h in range(N_HEADS):
        qth = qt_ref[heads(h), :]
        zero = jnp.zeros_like(qth)
        q2t_ref[h, 0:HEAD_COLS] = jnp.concatenate(
            [jnp.where(row < QK_DIM, qth, zero), jnp.where(row >= QK_DIM, qth, zero)], axis=1)
        if has_main:
            s1, s2, s3 = _bf16_parts(slope(h))
            ext = jnp.where(xrow == 0, s1, jnp.where(xrow == 1, s2, jnp.where(xrow == 2, s3,
                            jnp.where(xrow < 6, neg_q, 0.0))))
            q2t_ref[h, HEAD_COLS:HEAD_COLS + BIAS_ROWS] = ext.astype(BF16)
            q2t_ref[h, HEAD_COLS + BIAS_ROWS:] = jnp.zeros((HEAD_COLS - BIAS_ROWS, w2), BF16)
            ka = jnp.where(kcol < 3, kloc, jnp.where(kcol == 3, s1, jnp.where(kcol == 4, s2,
                           jnp.where(kcol == 5, s3, 0.0))))
            ka_ref[h] = ka.astype(BF16)
    m_ref[...] = jnp.full(m_ref.shape, NEG_INF, F32)
    l_ref[...] = jnp.zeros(l_ref.shape, F32)
    acc_ref[...] = jnp.zeros(acc_ref.shape, F32)

    def score_pass(h, width, score):
        z = score(h)
        z_ref[h, 0:width] = z
        zmax_ref[h] = jnp.max(z, axis=0, keepdims=True)

    def value_pass(h, width, vt_blk, c):
        m_old = m_ref[h]
        m_new = jnp.maximum(m_old, zmax_ref[h] + c(h))
        alpha = jnp.exp2(m_old - m_new)
        p = jnp.exp2(z_ref[h, 0:width] - (m_new - c(h))).astype(BF16)
        ones = jnp.ones((16, width), BF16)
        l_ref[h] = alpha * l_ref[h] + jnp.dot(ones, p, preferred_element_type=F32)[0:1]
        acc_ref[h] = alpha * acc_ref[h] + jnp.dot(vt_blk(h), p, preferred_element_type=F32)
        m_ref[h] = m_new

    shape_m = (N_META, w2)
    kj = lax.broadcasted_iota(jnp.int32, shape_m, 0)
    qpos = q_local(shape_m) + (pos_base0 + qi * TQ)
    vis_m = kj <= qpos
    dist_m = (qpos - kj).astype(F32)

    def meta_score(h):
        s = jnp.dot(km_ref[0, :, heads(h)], q2t_ref[h, 0:HEAD_COLS], preferred_element_type=F32)
        return s + jnp.where(vis_m, dist_m * -slope(h), NEG_INF)

    def key_block(width, score, vt_blk, c):
        for h in range(N_HEADS):
            score_pass(h, width, score)
        for h in range(N_HEADS):
            value_pass(h, width, vt_blk, c)

    key_block(N_META, meta_score, lambda h: vtm_ref[0, heads(h), :], lambda h: 0.0)

    if has_main:
        n_full = (qi * TQ) // TK

        def main_block(kb, masked):
            start = pl.multiple_of(kb * TK, TK)
            off = qi * TQ - kb * TK
            if masked:
                shape = (TK, w2)
                rel = q_local(shape) - lax.broadcasted_iota(jnp.int32, shape, 0)
                causal = (rel + off) >= 0

            def score(h):
                k_aug = jnp.concatenate([k_ref[pl.ds(start, TK), heads(h)], ka_ref[h]], axis=1)
                z = jnp.dot(k_aug, q2t_ref[h], preferred_element_type=F32)
                return jnp.where(causal, z, NEG_INF) if masked else z

            off_f = off.astype(F32)
            key_block(TK, score, lambda h: vt_ref[0, kb, heads(h), :], lambda h: off_f * -slope(h))

        def full_block(kb, carry):
            main_block(kb, False)
            return carry

        lax.fori_loop(0, n_full, full_block, 0)
        main_block(n_full, True)

    lam = _lambda(lamp_ref, lam_init)
    g_col = jnp.broadcast_to(g_ref[...], (V_DIM, TQ)) * (1.0 - lam_init)
    for h in range(N_HEADS):
        acc = acc_ref[h]
        l = l_ref[h]
        ot = acc[:, :TQ] / l[:, :TQ] - lam * (acc[:, TQ:] / l[:, TQ:])
        r = lax.rsqrt(jnp.mean(ot * ot, axis=0, keepdims=True) + EPS)
        o_ref[:, h * V_DIM:(h + 1) * V_DIM] = ((ot * r) * g_col).T.astype(BF16)


def _attention(qt, k_meta, vt_meta, k_main, vt_main, lamp, g_col, *, n_seq, rows_per_seq, pos_base0, lam_init,
               layer):
    nq = rows_per_seq // TQ
    has_main = k_main is not None
    in_specs = [
        pl.BlockSpec((ATTN_W, TQ), lambda b, i: (0, b * nq + i)),
        pl.BlockSpec((1, N_META, ATTN_W), lambda b, i: (b, 0, 0)),
        pl.BlockSpec((1, ATTN_W, N_META), lambda b, i: (b, 0, 0)),
    ]
    args = [qt, k_meta, vt_meta]
    scratch = [pltpu.VMEM((N_HEADS, 2 * HEAD_COLS if has_main else HEAD_COLS, 2 * TQ), BF16),
               pltpu.VMEM((N_HEADS, TK if has_main else N_META, 2 * TQ), F32),
               pltpu.VMEM((N_HEADS, 1, 2 * TQ), F32)]
    if has_main:
        assert rows_per_seq % TK == 0 and TK % TQ == 0
        in_specs += [
            pl.BlockSpec((rows_per_seq, ATTN_W), lambda b, i: (b, 0)),
            pl.BlockSpec((1, rows_per_seq // TK, ATTN_W, TK), lambda b, i: (b, 0, 0, 0)),
        ]
        args += [k_main, vt_main]
        scratch += [pltpu.VMEM((N_HEADS, TK, HEAD_COLS), BF16)]
    in_specs += [_layer_weight(lamp, layer), _layer_weight(g_col, layer)]
    args += [lamp, g_col]
    scratch += [
        pltpu.VMEM((N_HEADS, 1, 2 * TQ), F32),
        pltpu.VMEM((N_HEADS, 1, 2 * TQ), F32),
        pltpu.VMEM((N_HEADS, V_DIM, 2 * TQ), F32),
    ]
    return pl.pallas_call(
        functools.partial(_attn_kernel, has_main=has_main, pos_base0=pos_base0, lam_init=lam_init),
        grid=(n_seq, nq),
        in_specs=in_specs,
        out_specs=pl.BlockSpec((TQ, ATTN_W), lambda b, i: (b * nq + i, 0)),
        out_shape=jax.ShapeDtypeStruct((n_seq * rows_per_seq, ATTN_W), BF16),
        scratch_shapes=scratch,
        compiler_params=_params(("parallel", "parallel")),
        name="attn_prompt" if has_main else "attn_meta",
    )(*args)


def _sample_attn_kernel(pt_ref, q_ref, ktn_ref, vn_ref, *rest, pp, t_dec, past_len, lam_init):
    kt_refs = rest[:pp]
    v_refs = rest[pp:2 * pp]
    lamp_ref, g_ref, o_ref, qbd_ref, m_ref, l_ref, acc_ref = rest[2 * pp:]
    s_idx = pl.program_id(1)
    rows = N_HEADS * 2 * t_dec
    grp = 2 * t_dec

    r_col = lax.broadcasted_iota(jnp.int32, (rows, 1), 0)
    head = r_col // grp
    t_of_r = (r_col % t_dec).astype(F32)
    slope = lax.bitcast_convert_type((126 - head) << 23, F32) * LOG2E

    def pv_update(z, v_of_head):
        m_old = m_ref[...]
        m_new = jnp.maximum(m_old, jnp.max(z, axis=-1, keepdims=True))
        alpha = jnp.exp2(m_old - m_new)
        p = jnp.exp2(z - m_new)
        l_ref[...] = alpha * l_ref[...] + jnp.sum(p, axis=-1, keepdims=True)
        pb = p.astype(BF16)
        pv = [jnp.dot(pb[hh * grp:(hh + 1) * grp], v_of_head(hh), preferred_element_type=F32)
              for hh in range(N_HEADS)]
        acc_ref[...] = alpha * acc_ref[...] + jnp.concatenate(pv, axis=0)
        m_ref[...] = m_new

    @pl.when(s_idx == 0)
    def _init():
        qf = q_ref[0].astype(F32)
        qt = jnp.concatenate([qf] * (2 * N_HEADS), axis=0)
        rr = lax.broadcasted_iota(jnp.int32, qt.shape, 0) // t_dec
        cc = lax.broadcasted_iota(jnp.int32, qt.shape, 1) // QK_DIM
        qbd_ref[...] = jnp.where(rr == cc, qt, 0.0).astype(BF16)
        m_ref[...] = jnp.full(m_ref.shape, NEG_INF, F32)
        l_ref[...] = jnp.zeros(l_ref.shape, F32)
        acc_ref[...] = jnp.zeros(acc_ref.shape, F32)
        s = jnp.dot(qbd_ref[...], ktn_ref[0], preferred_element_type=F32)
        kj = lax.broadcasted_iota(jnp.int32, s.shape, 1)
        tq_i = lax.broadcasted_iota(jnp.int32, s.shape, 0) % t_dec
        dist = (tq_i - kj).astype(F32)
        z = jnp.where((kj < t_dec) & (kj <= tq_i), s - slope * dist, NEG_INF)
        vn = vn_ref[0]
        pv_update(z, lambda hh: vn[:, hh * V_DIM:(hh + 1) * V_DIM])

    qbd = qbd_ref[...]
    s = jnp.concatenate(
        [jnp.dot(qbd, kt_refs[j][0, 0].astype(BF16), preferred_element_type=F32) for j in range(pp)],
        axis=-1)
    col = lax.broadcasted_iota(jnp.int32, s.shape, 1).astype(F32)
    base = (s_idx * (pp * PAGE)).astype(F32)
    dist = (past_len + t_of_r - base) - col
    z = s - slope * dist

    def v_cached(hh):
        return jnp.concatenate(
            [v_refs[j][0, 0, pl.ds(hh, PAGE, stride=N_HEADS), :].astype(BF16) for j in range(pp)], axis=0)

    pv_update(z, v_cached)

    @pl.when(s_idx == pl.num_programs(1) - 1)
    def _fin():
        lam = _lambda(lamp_ref, lam_init)
        acc = acc_ref[...]
        l = l_ref[...]
        g = g_ref[...]
        outs = []
        for hh in range(N_HEADS):
            a0, a1 = hh * grp, hh * grp + t_dec
            o = acc[a0:a0 + t_dec] / l[a0:a0 + t_dec] - lam * (acc[a1:a1 + t_dec] / l[a1:a1 + t_dec])
            outs.append(_rms(o, g) * (1.0 - lam_init))
        o_ref[0] = jnp.concatenate(outs, axis=-1).astype(BF16)


def _sample_attention(page_table, q_s, kt_new, v_new, kt_cache, v_cache, lamp, g_subln, *, layer, pp,
                      lam_init):
    bd, t_dec, _ = q_s.shape
    n_pages = page_table.shape[1]
    assert n_pages % pp == 0
    rows = N_HEADS * 2 * t_dec

    def page_spec(j):
        return pl.BlockSpec((1, 1, ATTN_W, PAGE), lambda b, s, pt: (layer, pt[b, s * pp + j], 0, 0))

    in_specs = [
        pl.BlockSpec((1, t_dec, ATTN_W), lambda b, s, pt: (b, 0, 0)),
        pl.BlockSpec((1, ATTN_W, PAGE), lambda b, s, pt: (b, 0, 0)),
        pl.BlockSpec((1, PAGE, ATTN_W), lambda b, s, pt: (b, 0, 0)),
    ]
    in_specs += [page_spec(j) for j in range(pp)] + [page_spec(j) for j in range(pp)]
    in_specs += [_layer_weight(lamp, layer), _layer_weight(g_subln, layer)]
    grid_spec = pltpu.PrefetchScalarGridSpec(
        num_scalar_prefetch=1,
        grid=(bd, n_pages // pp),
        in_specs=in_specs,
        out_specs=pl.BlockSpec((1, t_dec, ATTN_W), lambda b, s, pt: (b, 0, 0)),
        scratch_shapes=[
            pltpu.VMEM((rows, ATTN_W), BF16),
            pltpu.VMEM((rows, 1), F32),
            pltpu.VMEM((rows, 1), F32),
            pltpu.VMEM((rows, V_DIM), F32),
        ],
    )
    return pl.pallas_call(
        functools.partial(_sample_attn_kernel, pp=pp, t_dec=t_dec, past_len=float(n_pages * PAGE),
                          lam_init=lam_init),
        grid_spec=grid_spec,
        out_shape=jax.ShapeDtypeStruct((bd, t_dec, ATTN_W), BF16),
        compiler_params=_params(("parallel", "arbitrary")),
        name="attn_sample",
    )(page_table, q_s, kt_new, v_new, *([kt_cache] * pp), *([v_cache] * pp), lamp, g_subln)


def _seq_mixers(zf_ref, ucf_ref, t, z, u, gate_b, gate_c, wpool_ref, pscale, cw, ramp_cnt):
    zf_ref[HIST:HIST + t] = z
    uc = gate_c * u
    ucf_ref[HIST:HIST + t] = uc
    pooled_out = []
    for gi, w in enumerate(POOL_WINDOWS):
        cols = pl.ds(gi * POOL_GROUP, POOL_GROUP)
        zg = z[:, gi * POOL_GROUP:(gi + 1) * POOL_GROUP]
        acc = zg
        for j in range(1, w):
            acc = acc + zf_ref[pl.ds(HIST - j, t), cols]
        if ramp_cnt:
            pos = lax.broadcasted_iota(jnp.int32, (t, 1), 0)
            cnt = jnp.minimum(pos + 1, w).astype(F32)
            pooled = acc / cnt - zg
        else:
            pooled = acc * (1.0 / w) - zg
        pooled_out.append(jnp.dot(pooled.astype(BF16), wpool_ref[gi], preferred_element_type=F32))
    o_pool = jnp.concatenate(pooled_out, axis=-1) * pscale
    y = cw[0:1] * ucf_ref[pl.ds(HIST - 2, t), :] + cw[1:2] * ucf_ref[pl.ds(HIST - 1, t), :] + cw[2:3] * uc
    return o_pool, gate_b * y, uc


def _mix_project(x, o_attn, o_pool, o_conv, wo_ref, g_post):
    mix = jnp.dot(o_attn, wo_ref[0:ATTN_W], preferred_element_type=F32)
    mix += jnp.dot(o_pool.astype(BF16), wo_ref[ATTN_W:ATTN_W + POOL_W], preferred_element_type=F32)
    mix += jnp.dot(o_conv.astype(BF16), wo_ref[ATTN_W + POOL_W:], preferred_element_type=F32)
    return x + _rms(mix, g_post)


def _mix_main_kernel(x_ref, oa_ref, zpu_ref, gat_ref, hz_ref, hg_ref, wo_ref, wpool_ref, ps_ref, cw_ref,
                     g_ref, xo_ref, uct_ref, zf_ref, ucf_ref, *, tm, tiles_per_seq):
    i = pl.program_id(0)

    @pl.when(i % tiles_per_seq == 0)
    def _():
        hz = hz_ref[...]
        hg = hg_ref[...]
        zf_ref[0:HIST] = hz[:, :POOL_W]
        ucf_ref[0:HIST] = hg[:, CONV_CH:] * hz[:, POOL_W:]

    @pl.when(i % tiles_per_seq != 0)
    def _():
        zf_ref[0:HIST] = zf_ref[tm:tm + HIST]
        ucf_ref[0:HIST] = ucf_ref[tm:tm + HIST]

    zpu = zpu_ref[...]
    gat = gat_ref[...]
    o_pool, o_conv, uc = _seq_mixers(zf_ref, ucf_ref, tm, zpu[:, :POOL_W], zpu[:, POOL_W:], gat[:, :CONV_CH],
                                     gat[:, CONV_CH:], wpool_ref, ps_ref[...], cw_ref[...], False)
    uct_ref[0] = uc[tm - HIST:]
    xo_ref[...] = _mix_project(x_ref[...], oa_ref[...], o_pool, o_conv, wo_ref, g_ref[...])


def _mix_main(x, o_attn, zpu, gates, hist_zpu, hist_gates, wo, wpool, pscale, cw, g_post, *, tm, rows_per_seq,
              layer):
    n, d = x.shape
    tps = rows_per_seq // tm
    n_seq = n // rows_per_seq
    row = lambda i: (i, 0)
    const2 = lambda i: (0, 0)
    return pl.pallas_call(
        functools.partial(_mix_main_kernel, tm=tm, tiles_per_seq=tps),
        grid=(n // tm,),
        in_specs=[
            pl.BlockSpec((tm, d), row),
            pl.BlockSpec((tm, ATTN_W), row),
            pl.BlockSpec((tm, ATTN_W), row),
            pl.BlockSpec((tm, ATTN_W), row),
            pl.BlockSpec((HIST, ATTN_W), lambda i: (i // tps, 0)),
            pl.BlockSpec((HIST, ATTN_W), lambda i: (i // tps, 0)),
            _resident(wo.shape, const2),
            _layer_weight(wpool, layer),
            _layer_weight(pscale, layer),
            _layer_weight(cw, layer),
            _layer_weight(g_post, layer),
        ],
        out_specs=(
            pl.BlockSpec((tm, d), row),
            pl.BlockSpec((1, HIST, CONV_CH), lambda i: (i // tps, 0, 0)),
        ),
        out_shape=(
            jax.ShapeDtypeStruct((n, d), F32),
            jax.ShapeDtypeStruct((n_seq, HIST, CONV_CH), F32),
        ),
        scratch_shapes=[pltpu.VMEM((HIST + tm, POOL_W), F32), pltpu.VMEM((HIST + tm, CONV_CH), F32)],
        compiler_params=_params(("arbitrary",)),
        name="mix_main",
    )(x, o_attn, zpu, gates, hist_zpu, hist_gates, wo, wpool, pscale, cw, g_post)


def _mix_small_kernel(x_ref, oa_ref, zpu_ref, gat_ref, hp_ref, hc_ref, wo_ref, wpool_ref, ps_ref, cw_ref,
                      g_ref, xo_ref, uc_ref, wob_ref, zf_ref, ucf_ref, *, n_meta_seq, n_dec, t_dec):
    wob_ref[...] = wo_ref[...].astype(BF16)
    wo_ref = wob_ref
    pools, convs, ucs = [], [], []
    segs = [(s * N_META, N_META, None) for s in range(n_meta_seq)]
    segs += [(n_meta_seq * N_META + b * t_dec, t_dec, b) for b in range(n_dec)]
    for start, t, b in segs:
        if b is None:
            zf_ref[0:HIST] = jnp.zeros((HIST, POOL_W), F32)
            ucf_ref[0:HIST] = jnp.zeros((HIST, CONV_CH), F32)
        else:
            zf_ref[0:HIST] = hp_ref[b]
            ucf_ref[0:HIST] = hc_ref[b]
        zpu = zpu_ref[start:start + t]
        gat = gat_ref[start:start + t]
        o_pool, o_conv, uc = _seq_mixers(zf_ref, ucf_ref, t, zpu[:, :POOL_W], zpu[:, POOL_W:], gat[:, :CONV_CH],
                                         gat[:, CONV_CH:], wpool_ref, ps_ref[...], cw_ref[...], b is None)
        pools.append(o_pool)
        convs.append(o_conv)
        ucs.append(uc)
    n_pad = x_ref.shape[0] - (n_meta_seq * N_META + n_dec * t_dec)
    if n_pad:
        pad = jnp.zeros((n_pad, POOL_W), F32)
        pools.append(pad)
        convs.append(pad)
        ucs.append(pad)
    uc_ref[...] = jnp.concatenate(ucs, axis=0)
    xo_ref[...] = _mix_project(x_ref[...], oa_ref[...], jnp.concatenate(pools, axis=0),
                               jnp.concatenate(convs, axis=0), wo_ref, g_ref[...])


def _mix_small(x, o_attn, zpu, gates, hist_pool, hist_conv, wo, wpool, pscale, cw, g_post, *, n_meta_seq,
               n_dec, t_dec, layer):
    n, d = x.shape
    whole = lambda a: pl.BlockSpec(a.shape, lambda i: (0,) * a.ndim)
    return pl.pallas_call(
        functools.partial(_mix_small_kernel, n_meta_seq=n_meta_seq, n_dec=n_dec, t_dec=t_dec),
        grid=(1,),
        in_specs=[whole(x), whole(o_attn), whole(zpu), whole(gates),
                  _layer_weight(hist_pool, layer), _layer_weight(hist_conv, layer),
                  _layer_weight(wo, layer), _layer_weight(wpool, layer), _layer_weight(pscale, layer), _layer_weight(cw, layer), _layer_weight(g_post, layer)],
        out_specs=(pl.BlockSpec((n, d), lambda i: (0, 0)), pl.BlockSpec((n, CONV_CH), lambda i: (0, 0)),
                   pl.BlockSpec(wo.shape[1:], lambda i: (0, 0))),
        out_shape=(jax.ShapeDtypeStruct((n, d), F32), jax.ShapeDtypeStruct((n, CONV_CH), F32),
                   jax.ShapeDtypeStruct(wo.shape[1:], BF16)),
        scratch_shapes=[pltpu.VMEM((HIST + N_META, POOL_W), F32), pltpu.VMEM((HIST + N_META, CONV_CH), F32)],
        compiler_params=_params(("arbitrary",)),
        name="mix_small",
    )(x, o_attn, zpu, gates, hist_pool, hist_conv, wo, wpool, pscale, cw, g_post)


def _ffn_kernel(x_ref, gpre_ref, wg_ref, wu_ref, wd_ref, gpost_ref, o_ref, *rest, emit_bf16):
    if emit_bf16:
        wg_out, wu_out, wd_out, h_ref, acc_ref = rest
        wg_out[...] = wg_ref[...].astype(BF16)
        wu_out[...] = wu_ref[...].astype(BF16)
        wd_out[...] = wd_ref[...].astype(BF16)
        wg_ref, wu_ref, wd_ref = wg_out, wu_out, wd_out
    else:
        h_ref, acc_ref = rest
    c = pl.program_id(1)

    @pl.when(c == 0)
    def _():
        h_ref[...] = _rms(x_ref[...], gpre_ref[...]).astype(BF16)
        acc_ref[...] = jnp.zeros(acc_ref.shape, F32)

    h = h_ref[...]
    gate = jnp.dot(h, wg_ref[...], preferred_element_type=F32)
    up = jnp.dot(h, wu_ref[...], preferred_element_type=F32)
    act = (gate * jax.nn.sigmoid(gate) * up).astype(BF16)
    acc_ref[...] += jnp.dot(act, wd_ref[...], preferred_element_type=F32)

    @pl.when(c == pl.num_programs(1) - 1)
    def _():
        o_ref[...] = x_ref[...] + _rms(acc_ref[...], gpost_ref[...])


def _ffn(x, g_pre, w_gate, w_up, w_down, g_post, *, tm, tc, layer):
    n, d = x.shape
    emit_bf16 = w_down.ndim == 3
    hidden = w_down.shape[-2]
    assert n % tm == 0 and hidden % tc == 0
    nc = hidden // tc
    if emit_bf16:
        w_specs = [
            pl.BlockSpec((None, d, tc), lambda i, c: (layer, 0, c)),
            pl.BlockSpec((None, d, tc), lambda i, c: (layer, 0, nc + c)),
            pl.BlockSpec((None, tc, d), lambda i, c: (layer, c, 0)),
        ]
    else:
        w_specs = [
            pl.BlockSpec((d, tc), lambda i, c: (0, c)),
            pl.BlockSpec((d, tc), lambda i, c: (0, c)),
            pl.BlockSpec((tc, d), lambda i, c: (c, 0)),
        ]
    out_specs = [pl.BlockSpec((tm, d), lambda i, c: (i, 0))]
    out_shape = [jax.ShapeDtypeStruct((n, d), F32)]
    if emit_bf16:
        assert n == tm
        out_specs += [
            pl.BlockSpec((d, tc), lambda i, c: (0, c)),
            pl.BlockSpec((d, tc), lambda i, c: (0, c)),
            pl.BlockSpec((tc, d), lambda i, c: (c, 0)),
        ]
        out_shape += [jax.ShapeDtypeStruct((d, hidden), BF16), jax.ShapeDtypeStruct((d, hidden), BF16),
                      jax.ShapeDtypeStruct((hidden, d), BF16)]
    res = pl.pallas_call(
        functools.partial(_ffn_kernel, emit_bf16=emit_bf16),
        grid=(n // tm, nc),
        in_specs=[pl.BlockSpec((tm, d), lambda i, c: (i, 0)), _layer_weight(g_pre, layer)] + w_specs
                 + [_layer_weight(g_post, layer)],
        out_specs=out_specs,
        out_shape=out_shape,
        scratch_shapes=[pltpu.VMEM((tm, d), BF16), pltpu.VMEM((tm, d), F32)],
        compiler_params=_params(("parallel", "arbitrary")),
        name="ffn_cast" if emit_bf16 else "ffn",
    )(x, g_pre, w_gate, w_up, w_down, g_post)
    return res if emit_bf16 else res[0]


TQ = 256
TK = 256
TM_MIX = 512
TM_FFN = 512
TC_FFN = 512
PAGES_PER_STEP = 16


def kernel(x_prompt, x_sample, cache_k, cache_v, state_pool, state_conv, page_table, meta_tokens, w_in, w_o,
           lambda_q1, lambda_k1, lambda_q2, lambda_k2, g_subln, w_pool, pool_scale, conv_w, w_gate_up, w_down,
           g_pre_mix, g_post_mix, g_pre_ffn, g_post_ffn):
    nb, seq, d = x_prompt.shape
    bd, t_dec, _ = x_sample.shape
    depth = w_in.shape[0]
    n_pool = cache_k.shape[1]
    n_small = nb * N_META + bd * t_dec
    meta_rows = nb * N_META
    ns_pad = -(-n_small // PAGE) * PAGE

    kt_cache = jnp.transpose(cache_k, (0, 1, 3, 4, 5, 2)).reshape(depth, n_pool, ATTN_W, PAGE)
    v_cache = cache_v.reshape(depth, n_pool, PAGE * N_HEADS, V_DIM)

    xm = x_prompt.reshape(nb * seq, d)
    xs = jnp.concatenate([jnp.broadcast_to(meta_tokens[None], (nb, N_META, d)).reshape(meta_rows, d),
                          x_sample.reshape(bd * t_dec, d), jnp.zeros((ns_pad - n_small, d), F32)], axis=0)

    w_in_b, w_pool_b = w_in.astype(BF16), w_pool.astype(BF16)

    lamp = jnp.stack([lambda_q1, lambda_k1, lambda_q2, lambda_k2], axis=1)
    row = lambda a: a[:, None, :]
    g_sub = row(g_subln)
    g_col = g_subln[:, :, None]
    pscale = row(pool_scale)
    cw = conv_w
    g_pre_mix, g_post_mix, g_pre_ffn, g_post_ffn = (row(g) for g in (g_pre_mix, g_post_mix, g_pre_ffn, g_post_ffn))
    hist_pool = jnp.pad(state_pool, ((0, 0), (0, 0), (HIST - state_pool.shape[2], 0), (0, 0)))
    hist_conv = jnp.pad(state_conv, ((0, 0), (0, 0), (HIST - state_conv.shape[2], 0), (0, 0)))

    outs = {k: [] for k in ("kpm", "kp", "vp", "pp", "cp", "ks", "vs", "ps", "cs")}
    for l in range(depth):
        lam_init = 0.8 - 0.6 * math.exp(-0.3 * l)

        qt_m, kb_m, ktf_m, vf_m, vtb_m, zpu_m, gat_m = _inproj(xm, g_pre_mix, w_in_b, TK, seq, l)
        qt_s, kb_s, ktf_s, vf_s, vtb_s, zpu_s, gat_s = _inproj(xs, g_pre_mix, w_in_b, ns_pad, ns_pad, l)
        ktf_s, vtb_s = ktf_s[0], vtb_s[0, 0]

        k_meta = kb_s[:meta_rows].reshape(nb, N_META, ATTN_W)
        vt_meta = vtb_s[:, :meta_rows].reshape(ATTN_W, nb, N_META).transpose(1, 0, 2)
        attn = functools.partial(_attention, lamp=lamp, g_col=g_col, n_seq=nb, lam_init=lam_init, layer=l)
        oa_m = attn(qt_m, k_meta, vt_meta, kb_m, vtb_m, rows_per_seq=seq, pos_base0=N_META)
        qt_meta = jnp.pad(qt_s[:, :meta_rows].reshape(ATTN_W, nb, N_META), ((0, 0), (0, 0), (0, TQ - N_META)))
        oa_meta = attn(qt_meta.reshape(ATTN_W, nb * TQ), k_meta, vt_meta, None, None, rows_per_seq=TQ, pos_base0=0)
        oa_meta = oa_meta.reshape(nb, TQ, ATTN_W)[:, :N_META].reshape(meta_rows, ATTN_W)

        q_dec = qt_s[:, meta_rows:n_small].T.reshape(bd, t_dec, ATTN_W)
        kt_new = jnp.pad(ktf_s[:, meta_rows:n_small].astype(BF16).reshape(ATTN_W, bd, t_dec).transpose(1, 0, 2),
                         ((0, 0), (0, 0), (0, PAGE - t_dec)))
        v_new = jnp.pad(vf_s[meta_rows:n_small].astype(BF16).reshape(bd, t_dec, ATTN_W),
                        ((0, 0), (0, PAGE - t_dec), (0, 0)))
        oa_dec = _sample_attention(page_table, q_dec, kt_new, v_new, kt_cache, v_cache, lamp, g_sub, layer=l,
                                   pp=math.gcd(PAGES_PER_STEP, page_table.shape[1]), lam_init=lam_init)
        oa_s = jnp.concatenate([oa_meta, oa_dec.reshape(bd * t_dec, ATTN_W),
                                jnp.zeros((ns_pad - n_small, ATTN_W), BF16)], axis=0)

        assert N_META == HIST
        xs, uc_s, w_o_l = _mix_small(xs, oa_s, zpu_s, gat_s, hist_pool, hist_conv, w_o, w_pool_b, pscale, cw,
                                     g_post_mix, n_meta_seq=nb, n_dec=bd, t_dec=t_dec, layer=l)
        xm, uc_tail = _mix_main(xm, oa_m, zpu_m, gat_m, zpu_s, gat_s, w_o_l, w_pool_b, pscale, cw,
                                g_post_mix, tm=min(TM_MIX, seq), rows_per_seq=seq, layer=l)

        xs, w_g_l, w_u_l, w_d_l = _ffn(xs, g_pre_ffn, w_gate_up, w_gate_up, w_down, g_post_ffn, tm=ns_pad,
                                       tc=TC_FFN, layer=l)
        xm = _ffn(xm, g_pre_ffn, w_g_l, w_u_l, w_d_l, g_post_ffn, tm=min(TM_FFN, seq), tc=TC_FFN, layer=l)

        outs["kpm"].append(ktf_s[:, :meta_rows].reshape(ATTN_W, nb, N_META).transpose(1, 0, 2))
        outs["kp"].append(ktf_m)
        v_p = jnp.concatenate([vf_s[:meta_rows].reshape(nb, N_META, ATTN_W), vf_m.reshape(nb, seq, ATTN_W)], axis=1)
        outs["vp"].append(v_p.reshape(nb, N_META + seq, N_HEADS, V_DIM))
        z_m = zpu_m.reshape(nb, seq, ATTN_W)
        n_ph = state_pool.shape[2]
        n_ch = state_conv.shape[2]
        outs["pp"].append(z_m[:, seq - n_ph:, :POOL_W])
        outs["cp"].append(uc_tail[:, HIST - n_ch:])
        outs["ks"].append(ktf_s[:, meta_rows:n_small].T.reshape(bd, t_dec, N_HEADS, 2, QK_DIM))
        outs["vs"].append(vf_s[meta_rows:n_small].reshape(bd, t_dec, N_HEADS, V_DIM))
        z_dec = zpu_s[meta_rows:n_small, :POOL_W].reshape(bd, t_dec, POOL_W)
        outs["ps"].append(jnp.concatenate([state_pool[l], z_dec], axis=1)[:, -n_ph:])
        outs["cs"].append(uc_s[meta_rows:n_small].reshape(bd, t_dec, CONV_CH)[:, -n_ch:])

    st = lambda k: jnp.stack(outs[k])
    kt_p = jnp.concatenate([st("kpm"), st("kp")], axis=3)
    k_p = kt_p.reshape(depth, nb, N_HEADS, 2, QK_DIM, N_META + seq).transpose(0, 1, 5, 2, 3, 4)
    return (xm.reshape(nb, seq, d), xs[meta_rows:n_small].reshape(bd, t_dec, d),
            k_p, st("vp"), st("pp"), st("cp"), st("ks"), st("vs"), st("ps"), st("cs"))
```
